```python
import math
import jax
import jax.numpy as jnp
from jax import lax
import numpy as np

D_MODEL = 4096
BATCH = 4
SEQ = 2048
DEPTH = 2
DEC_BATCH = 8
DEC_SEQ = 1
PAST_LEN = 16384
PAGE_SIZE = 128

HEAD_DIM = 128
W_GROUP = D_MODEL // 4
W_MIX = 4 * W_GROUP
H_A = W_GROUP // HEAD_DIM
GDN_CONV = 4
GDN_CHUNK = 64
SCONV = 3
H_C = W_GROUP // HEAD_DIM
Q_BLOCK = 128
FORGET_BIAS_INIT = 8.0
LRU_CONV = 4
LRU_BLOCKS = W_GROUP // HEAD_DIM
LRU_BW = W_GROUP // LRU_BLOCKS
LRU_C = 8.0
D_FF = ((8 * D_MODEL // 3 + 255) // 256) * 256
FFN_CONV = 3
EPS = 1e-6

OFF_A_QKV = 0
OFF_A_Z = OFF_A_QKV + 3 * W_GROUP
OFF_A_BETA = OFF_A_Z + W_GROUP
OFF_A_ALPHA = OFF_A_BETA + H_A
OFF_B = OFF_A_ALPHA + H_A
OFF_C = OFF_B + 3 * W_GROUP
OFF_C_F = OFF_C + 3 * W_GROUP
OFF_D = OFF_C_F + H_C
N_IN = OFF_D + 2 * W_GROUP

kernel_name = 'hybrid_parallel_groups_decode_step'


def rmsnorm(x, w):
    xf = x.astype(jnp.float32)
    xf = xf * lax.rsqrt(jnp.mean(xf * xf, axis=-1, keepdims=True) + EPS)
    return (xf * w.astype(jnp.float32)).astype(x.dtype)


def l2norm(x):
    xf = x.astype(jnp.float32)
    return xf * lax.rsqrt(jnp.sum(xf * xf, axis=-1, keepdims=True) + EPS)


def causal_dwconv(x, buf, w):
    width = w.shape[0]
    xp = jnp.concatenate([buf.astype(x.dtype), x], axis=1)
    y = lax.conv_general_dilated(xp, w[:, None, :].astype(x.dtype), window_strides=(1,),
                                 padding='VALID', dimension_numbers=('NWC', 'WIO', 'NWC'),
                                 feature_group_count=x.shape[-1])
    return y, xp[:, xp.shape[1] - (width - 1):]


def gdn_chunked(q, k, v, g, beta, s0):
    bsz, t, h, _ = q.shape
    dv = v.shape[-1]
    n = t // GDN_CHUNK

    def chunks(a):
        a = a.reshape((bsz, n, GDN_CHUNK, h) + a.shape[3:])
        return jnp.moveaxis(a, (1, 3), (0, 2))

    qc, kc, vc, gc, bc = (chunks(a) for a in (q, k, v, g, beta))
    gc = lax.cumsum(gc, axis=3)
    idx = jnp.arange(GDN_CHUNK)
    causal = idx[:, None] >= idx[None, :]
    strict = idx[:, None] > idx[None, :]
    diff = gc[..., :, None] - gc[..., None, :]
    decay = jnp.where(causal, jnp.exp(jnp.where(causal, diff, 0.0)), 0.0)
    kb = kc * bc[..., None]
    lower = jnp.einsum('nbhik,nbhjk->nbhij', kb, kc) * jnp.where(strict, decay, 0.0)
    a_mat = jnp.eye(GDN_CHUNK, dtype=jnp.float32) + lower
    rhs = jnp.concatenate([vc * bc[..., None], kb * jnp.exp(gc)[..., None]], axis=-1)
    sol = lax.linalg.triangular_solve(a_mat, rhs, left_side=True, lower=True, unit_diagonal=True)
    u, w = sol[..., :dv], sol[..., dv:]
    qk = jnp.einsum('nbhik,nbhjk->nbhij', qc, kc) * decay

    def step(s, xs):
        qi, ki, ui, wi, gi, qki = xs
        v_new = ui - jnp.einsum('bhck,bhkv->bhcv', wi, s)
        o = (jnp.einsum('bhck,bhkv->bhcv', qi * jnp.exp(gi)[..., None], s)
             + jnp.einsum('bhij,bhjv->bhiv', qki, v_new))
        g_last = gi[..., -1]
        k_dec = ki * jnp.exp(g_last[..., None] - gi)[..., None]
        s = s * jnp.exp(g_last)[..., None, None] + jnp.einsum('bhck,bhcv->bhkv', k_dec, v_new)
        return s, o

    s_final, o = lax.scan(step, s0, (qc, kc, u, w, gc, qk))
    o = jnp.moveaxis(o, (0, 2), (1, 3)).reshape(bsz, t, h, dv)
    return o, s_final


def gdn_recurrent(q, k, v, g, beta, s0):
    def step(s, xs):
        qt, kt, vt, gt, bt = xs
        s = s * jnp.exp(gt)[..., None, None]
        delta = (vt - jnp.einsum('bhk,bhkv->bhv', kt, s)) * bt[..., None]
        s = s + kt[..., :, None] * delta[..., None, :]
        return s, jnp.einsum('bhk,bhkv->bhv', qt, s)

    xs = tuple(jnp.moveaxis(a, 1, 0) for a in (q, k, v, g, beta))
    s_final, o = lax.scan(step, s0, xs)
    return jnp.moveaxis(o, 0, 1), s_final


def fox_prompt(q, k, v, logf):
    bsz, t, h, dh = q.shape
    nb = t // Q_BLOCK
    f_cum = jnp.swapaxes(lax.cumsum(logf, axis=1), 1, 2)
    q_blocks = jnp.moveaxis(q.reshape(bsz, nb, Q_BLOCK, h, dh), 1, 0)
    f_blocks = jnp.moveaxis(f_cum.reshape(bsz, h, nb, Q_BLOCK), 2, 0)
    kpos = jnp.arange(t)

    def attend_block(args):
        qb, fb, start = args
        s = jnp.einsum('bqhd,bkhd->bhqk', qb, k).astype(jnp.float32) * HEAD_DIM ** -0.5
        s = s + fb[..., :, None] - f_cum[:, :, None, :]
        qpos = start + jnp.arange(Q_BLOCK)
        s = jnp.where(kpos[None, :] <= qpos[:, None], s, -jnp.inf)
        p = jax.nn.softmax(s, axis=-1)
        return jnp.einsum('bhqk,bkhd->bqhd', p.astype(v.dtype), v)

    o = lax.map(attend_block, (q_blocks, f_blocks, jnp.arange(nb) * Q_BLOCK))
    return jnp.moveaxis(o, 0, 1).reshape(bsz, t, h, dh)


def fox_sample(q, k, v, logf, k_past, v_past, logf_past):
    bsz, t, h, dh = q.shape
    p = k_past.shape[1]
    keys = jnp.concatenate([k_past.astype(k.dtype), k], axis=1)
    vals = jnp.concatenate([v_past.astype(v.dtype), v], axis=1)
    lf = jnp.concatenate([logf_past.astype(jnp.float32), logf], axis=1)
    suffix = lax.cumsum(lf, axis=1, reverse=True)
    r = jnp.concatenate([suffix[:, 1:], jnp.zeros_like(suffix[:, :1])], axis=1)
    rt = jnp.swapaxes(r, 1, 2)
    s = jnp.einsum('bqhd,bkhd->bhqk', q, keys).astype(jnp.float32) * HEAD_DIM ** -0.5
    s = s + rt[:, :, None, :] - rt[:, :, p:, None]
    qpos = p + jnp.arange(t)
    kpos = jnp.arange(p + t)
    s = jnp.where(kpos[None, :] <= qpos[:, None], s, -jnp.inf)
    w = jax.nn.softmax(s, axis=-1)
    return jnp.einsum('bhqk,bkhd->bqhd', w.astype(vals.dtype), vals)


def block_diag(x, w, b):
    xb = x.reshape(x.shape[:-1] + (LRU_BLOCKS, LRU_BW))
    return jnp.einsum('btnk,nkj->btnj', xb, w.astype(x.dtype)).reshape(x.shape) + b.astype(x.dtype)


def lru_combine(left, right):
    a_l, b_l = left
    a_r, b_r = right
    return a_l * a_r, a_r * b_l + b_r


def rg_lru(xd, wr, br, wi, bi, lam, h0):
    xf = xd.astype(jnp.float32)
    r = jax.nn.sigmoid(block_diag(xf, wr, br))
    i = jax.nn.sigmoid(block_diag(xf, wi, bi))
    log_a = -LRU_C * r * jax.nn.softplus(-lam.astype(jnp.float32))
    a = jnp.exp(log_a)
    b = jnp.sqrt(-jnp.expm1(2.0 * log_a)) * (i * xf)
    b = b.at[:, 0].add(a[:, 0] * h0.astype(jnp.float32))
    _, h = lax.associative_scan(lru_combine, (a, b), axis=1)
    return h, h[:, -1]


def trunk_layer(x, c, prm, state, past):
    f32 = jnp.float32
    bsz, t, _ = x.shape
    gdn_s0, gdn_buf, sconv_buf, lru_h0, lru_buf, ffn_buf = state
    mod = (jax.nn.silu(c) @ prm['ada_w'] + prm['ada_b'])[:, None, :]
    shift1, scale1, gate1, shift2, scale2, gate2 = jnp.split(mod, 6, axis=-1)

    h = rmsnorm(x, prm['ln1_w']) * (1 + scale1) + shift1
    proj = h @ prm['w_in']

    qkv, gdn_buf_new = causal_dwconv(proj[..., OFF_A_QKV:OFF_A_Z], gdn_buf, prm['gdn_conv_w'])
    qkv = jax.nn.silu(qkv).reshape(bsz, t, 3, H_A, HEAD_DIM)
    qa = l2norm(qkv[:, :, 0]) * HEAD_DIM ** -0.5
    ka = l2norm(qkv[:, :, 1])
    va = qkv[:, :, 2].astype(f32)
    beta = jax.nn.sigmoid(proj[..., OFF_A_BETA:OFF_A_ALPHA].astype(f32))
    g = -jnp.exp(prm['gdn_a_log'].astype(f32)) * jax.nn.softplus(
        proj[..., OFF_A_ALPHA:OFF_B].astype(f32) + prm['gdn_dt_bias'].astype(f32))
    gdn_fn = gdn_chunked if past is None else gdn_recurrent
    oa, gdn_s = gdn_fn(qa, ka, va, g, beta, gdn_s0.astype(f32))
    za = proj[..., OFF_A_Z:OFF_A_BETA].reshape(bsz, t, H_A, HEAD_DIM).astype(f32)
    oa = (rmsnorm(oa, prm['gdn_norm_w']) * jax.nn.silu(za)).reshape(bsz, t, W_GROUP).astype(x.dtype)

    xb = proj[..., OFF_B:OFF_B + W_GROUP]
    gb = proj[..., OFF_B + W_GROUP:OFF_B + 2 * W_GROUP]
    gcb = proj[..., OFF_B + 2 * W_GROUP:OFF_C]
    conv_b, sconv_buf_new = causal_dwconv(gcb * xb, sconv_buf, prm['sconv_w'])
    ob = gb * conv_b

    qc = proj[..., OFF_C:OFF_C + W_GROUP].reshape(bsz, t, H_C, HEAD_DIM)
    kc = proj[..., OFF_C + W_GROUP:OFF_C + 2 * W_GROUP].reshape(bsz, t, H_C, HEAD_DIM)
    vc = proj[..., OFF_C + 2 * W_GROUP:OFF_C_F].reshape(bsz, t, H_C, HEAD_DIM)
    logf = jax.nn.log_sigmoid(proj[..., OFF_C_F:OFF_D].astype(f32) + prm['fox_f_bias'].astype(f32))
    if past is None:
        oc = fox_prompt(qc, kc, vc, logf)
    else:
        oc = fox_sample(qc, kc, vc, logf, past[0], past[1], past[2])
    oc = oc.reshape(bsz, t, W_GROUP).astype(x.dtype)

    xd, lru_buf_new = causal_dwconv(proj[..., OFF_D:OFF_D + W_GROUP], lru_buf, prm['lru_conv_w'])
    hd, lru_h = rg_lru(xd + prm['lru_conv_b'], prm['lru_wr'], prm['lru_br'], prm['lru_wi'],
                       prm['lru_bi'], prm['lru_lambda'], lru_h0)
    od = jax.nn.gelu(proj[..., OFF_D + W_GROUP:N_IN]) * hd.astype(x.dtype)

    mix = jnp.concatenate([oa, ob, oc, od], axis=-1) @ prm['w_out']
    x = x + gate1 * mix

    h2 = rmsnorm(x, prm['ln2_w']) * (1 + scale2) + shift2
    up = h2 @ prm['w_up']
    hg, ffn_buf_new = causal_dwconv(up[..., :D_FF], ffn_buf, prm['ffn_conv_w'])
    x = x + gate2 * ((jax.nn.silu(hg) * up[..., D_FF:]) @ prm['w_down'])

    new_state = (kc, vc, logf.astype(x.dtype), gdn_s.astype(x.dtype), gdn_buf_new,
                 sconv_buf_new, lru_h.astype(x.dtype), lru_buf_new, ffn_buf_new)
    return x, new_state


def stack_layers(per_layer):
    return tuple(jnp.stack(parts) for parts in zip(*per_layer))


def setup_inputs(seed: int = 0) -> dict:
    key = jax.random.key(seed)
    keys = iter(jax.random.split(key, 48))
    f32 = jnp.float32
    n_pages = PAST_LEN // PAGE_SIZE
    n_pool = (5 * DEC_BATCH * n_pages + 3) // 4

    def normal(shape, scale=1.0):
        return jax.random.normal(next(keys), shape, f32) * scale

    def uniform(shape, lo, hi):
        return jax.random.uniform(next(keys), shape, f32, lo, hi)

    def gain(shape):
        return 1.0 + normal(shape, 0.02)

    perm = jax.random.permutation(next(keys), n_pool).astype(jnp.int32)
    page_table = perm[:DEC_BATCH * n_pages].reshape(DEC_BATCH, n_pages)
    dt_init = jnp.exp(uniform((DEPTH, H_A), math.log(1e-3), math.log(1e-1)))
    a_init = uniform((DEPTH, W_GROUP), 0.9, 0.999)
    return {
        'x_prompt': normal((BATCH, SEQ, D_MODEL)),
        'x_sample': normal((DEC_BATCH, DEC_SEQ, D_MODEL)),
        'cache_k': normal((DEPTH, n_pool, PAGE_SIZE, H_C, HEAD_DIM)),
        'cache_v': normal((DEPTH, n_pool, PAGE_SIZE, H_C, HEAD_DIM)),
        'cache_logf': jax.nn.log_sigmoid(normal((DEPTH, n_pool, PAGE_SIZE, H_C)) + FORGET_BIAS_INIT),
        'state_gdn': normal((DEPTH, DEC_BATCH, H_A, HEAD_DIM, HEAD_DIM), 0.5),
        'state_gdn_conv': normal((DEPTH, DEC_BATCH, GDN_CONV - 1, 3 * W_GROUP)),
        'state_sconv': normal((DEPTH, DEC_BATCH, SCONV - 1, W_GROUP)),
        'state_lru_h': normal((DEPTH, DEC_BATCH, W_GROUP)),
        'state_lru_conv': normal((DEPTH, DEC_BATCH, LRU_CONV - 1, W_GROUP)),
        'state_ffn_conv': normal((DEPTH, DEC_BATCH, FFN_CONV - 1, D_FF)),
        'page_table': page_table,
        'c_prompt': normal((BATCH, D_MODEL)),
        'c_sample': normal((DEC_BATCH, D_MODEL)),
        'ada_w': normal((DEPTH, D_MODEL, 6 * D_MODEL), 0.5 * D_MODEL ** -0.5),
        'ada_b': normal((DEPTH, 6 * D_MODEL), 0.02),
        'ln1_w': gain((DEPTH, D_MODEL)),
        'ln2_w': gain((DEPTH, D_MODEL)),
        'w_in': normal((DEPTH, D_MODEL, N_IN), D_MODEL ** -0.5),
        'gdn_conv_w': normal((DEPTH, GDN_CONV, 3 * W_GROUP), GDN_CONV ** -0.5),
        'gdn_a_log': jnp.log(uniform((DEPTH, H_A), 1.0, 16.0)),
        'gdn_dt_bias': dt_init + jnp.log(-jnp.expm1(-dt_init)),
        'gdn_norm_w': gain((DEPTH, HEAD_DIM)),
        'sconv_w': normal((DEPTH, SCONV, W_GROUP), SCONV ** -0.5),
        'fox_f_bias': FORGET_BIAS_INIT + normal((DEPTH, H_C), 0.1),
        'lru_conv_w': normal((DEPTH, LRU_CONV, W_GROUP), LRU_CONV ** -0.5),
        'lru_conv_b': normal((DEPTH, W_GROUP), 0.02),
        'lru_wr': normal((DEPTH, LRU_BLOCKS, LRU_BW, LRU_BW), LRU_BW ** -0.5),
        'lru_br': normal((DEPTH, W_GROUP), 0.02),
        'lru_wi': normal((DEPTH, LRU_BLOCKS, LRU_BW, LRU_BW), LRU_BW ** -0.5),
        'lru_bi': normal((DEPTH, W_GROUP), 0.02),
        'lru_lambda': jnp.log(a_init) - jnp.log1p(-a_init),
        'w_out': normal((DEPTH, W_MIX, D_MODEL), W_MIX ** -0.5),
        'ffn_conv_w': normal((DEPTH, FFN_CONV, D_FF), FFN_CONV ** -0.5),
        'w_up': normal((DEPTH, D_MODEL, 2 * D_FF), D_MODEL ** -0.5),
        'w_down': normal((DEPTH, D_FF, D_MODEL), D_FF ** -0.5),
        'final_norm_w': gain((D_MODEL,)),
    }


def reference(x_prompt, x_sample, cache_k, cache_v, cache_logf, state_gdn, state_gdn_conv,
              state_sconv, state_lru_h, state_lru_conv, state_ffn_conv, page_table,
              c_prompt, c_sample, ada_w, ada_b, ln1_w, ln2_w, w_in, gdn_conv_w, gdn_a_log,
              gdn_dt_bias, gdn_norm_w, sconv_w, fox_f_bias, lru_conv_w, lru_conv_b, lru_wr,
              lru_br, lru_wi, lru_bi, lru_lambda, w_out, ffn_conv_w, w_up, w_down, final_norm_w):
    dt = x_prompt.dtype
    bp = x_prompt.shape[0]
    bs = x_sample.shape[0]
    past_len = page_table.shape[1] * cache_k.shape[2]
    xp, xs = x_prompt, x_sample
    p_states, s_states = [], []
    for l in range(DEPTH):
        prm = dict(ada_w=ada_w[l], ada_b=ada_b[l], ln1_w=ln1_w[l], ln2_w=ln2_w[l], w_in=w_in[l],
                   gdn_conv_w=gdn_conv_w[l], gdn_a_log=gdn_a_log[l], gdn_dt_bias=gdn_dt_bias[l],
                   gdn_norm_w=gdn_norm_w[l], sconv_w=sconv_w[l], fox_f_bias=fox_f_bias[l],
                   lru_conv_w=lru_conv_w[l], lru_conv_b=lru_conv_b[l], lru_wr=lru_wr[l],
                   lru_br=lru_br[l], lru_wi=lru_wi[l], lru_bi=lru_bi[l], lru_lambda=lru_lambda[l],
                   w_out=w_out[l], ffn_conv_w=ffn_conv_w[l], w_up=w_up[l], w_down=w_down[l])
        fresh = (jnp.zeros((bp, H_A, HEAD_DIM, HEAD_DIM), dt),
                 jnp.zeros((bp, GDN_CONV - 1, 3 * W_GROUP), dt),
                 jnp.zeros((bp, SCONV - 1, W_GROUP), dt),
                 jnp.zeros((bp, W_GROUP), dt),
                 jnp.zeros((bp, LRU_CONV - 1, W_GROUP), dt),
                 jnp.zeros((bp, FFN_CONV - 1, D_FF), dt))
        xp, st_p = trunk_layer(xp, c_prompt, prm, fresh, None)
        p_states.append(st_p)
        past = (cache_k[l, page_table].reshape(bs, past_len, H_C, HEAD_DIM),
                cache_v[l, page_table].reshape(bs, past_len, H_C, HEAD_DIM),
                cache_logf[l, page_table].reshape(bs, past_len, H_C))
        carried = (state_gdn[l], state_gdn_conv[l], state_sconv[l], state_lru_h[l],
                   state_lru_conv[l], state_ffn_conv[l])
        xs, st_s = trunk_layer(xs, c_sample, prm, carried, past)
        s_states.append(st_s)
    y_prompt = rmsnorm(xp, final_norm_w)
    y_sample = rmsnorm(xs, final_norm_w)
    (p_k, p_v, p_logf, p_gdn, p_gdn_conv, p_sconv, p_lru_h, p_lru_conv,
     p_ffn_conv) = stack_layers(p_states)
    (s_k, s_v, s_logf, s_gdn, s_gdn_conv, s_sconv, s_lru_h, s_lru_conv,
     s_ffn_conv) = stack_layers(s_states)
    return (y_prompt, y_sample,
            p_k, p_v, p_logf, p_gdn, p_gdn_conv, p_sconv, p_lru_h, p_lru_conv, p_ffn_conv,
            s_k, s_v, s_logf, s_gdn, s_gdn_conv, s_sconv, s_lru_h, s_lru_conv, s_ffn_conv)
```

```python
import functools

import jax
import jax.numpy as jnp
from jax import lax
from jax.experimental import pallas as pl
from jax.experimental.pallas import tpu as pltpu

F32 = jnp.float32
BF16 = jnp.bfloat16

D_MODEL = 4096
HEAD_DIM = 128
W_GROUP = D_MODEL // 4
N_HEADS = W_GROUP // HEAD_DIM
GDN_CHUNK = 64
D_FF = 11008
D_FF_PAD = 11264
LRU_C = 8.0
EPS = 1e-6
SAMPLE_ROWS = 16

OFF_A_BETA = 4 * W_GROUP
OFF_B = OFF_A_BETA + 2 * N_HEADS
OFF_C_F = OFF_B + 6 * W_GROUP
OFF_D = OFF_C_F + N_HEADS
N_IN = OFF_D + 2 * W_GROUP
M_A = 0
M_B = 4 * W_GROUP
M_C = 7 * W_GROUP
M_D = 10 * W_GROUP
N_MAIN = 12 * W_GROUP
L_BETA, L_G, L_F = 0, N_HEADS, 2 * N_HEADS

VMEM_LIMIT = 56 * 1024 * 1024

NN = ((1,), (0,))
NT = ((1,), (1,))
TN = ((0,), (0,))


def _cp(*sem):
    return pltpu.CompilerParams(dimension_semantics=sem, vmem_limit_bytes=VMEM_LIMIT)


def _dg(a, b, dims=NN):
    return lax.dot_general(a, b, (dims, ((), ())), preferred_element_type=F32)


def _mm1(a, b, dims=NN):
    return _dg(a.astype(BF16), b.astype(BF16), dims)


def _split(a):
    hi = a.astype(BF16)
    lo = (a - hi.astype(F32)).astype(BF16)
    return hi, lo


def _mm3(a, b, dims=NN):
    ah, al = _split(a)
    bh, bl = _split(b)
    return _dg(ah, bh, dims) + (_dg(ah, bl, dims) + _dg(al, bh, dims))


def _mm_exact_lhs(a_bf16, b):
    b0 = b.astype(BF16)
    r1 = b - b0.astype(F32)
    b1 = r1.astype(BF16)
    b2 = (r1 - b1.astype(F32)).astype(BF16)
    return _dg(a_bf16, b0) + (_dg(a_bf16, b1) + _dg(a_bf16, b2))


def _mm_exact_rhs(a, b_bf16):
    a0 = a.astype(BF16)
    r1 = a - a0.astype(F32)
    a1 = r1.astype(BF16)
    a2 = (r1 - a1.astype(F32)).astype(BF16)
    return _dg(a0, b_bf16) + (_dg(a1, b_bf16) + _dg(a2, b_bf16))


def _silu(x):
    return x * jax.nn.sigmoid(x)


def _iota(shape, axis):
    return lax.broadcasted_iota(jnp.int32, shape, axis)


def _shift_rows(x, s):
    r = pltpu.roll(x, s, axis=0)
    return jnp.where(_iota(x.shape, 0) < s, 0.0, r)


def _lane_pick(x, lane):
    return jnp.sum(jnp.where(_iota(x.shape, 1) == lane, x, 0.0), axis=-1, keepdims=True)


def _causal_conv(x, w):
    width = w.shape[0]
    y = x * w[width - 1:width, :]
    for s in range(1, width):
        y = y + _shift_rows(x, s) * w[width - 1 - s:width - s, :]
    return y


def _ada_kernel(c_ref, w_ref, b_ref, o_ref):
    c = c_ref[...]
    o_ref[...] = _mm1(_silu(c), w_ref[...]) + b_ref[...]


def _ada(c16, w, b):
    k, n = w.shape
    tn = 512
    return pl.pallas_call(
        _ada_kernel,
        grid=(n // tn,),
        in_specs=[pl.BlockSpec((SAMPLE_ROWS, k), lambda j: (0, 0)),
                  pl.BlockSpec((k, tn), lambda j: (0, j)),
                  pl.BlockSpec((1, tn), lambda j: (0, j))],
        out_specs=pl.BlockSpec((SAMPLE_ROWS, tn), lambda j: (0, j)),
        out_shape=jax.ShapeDtypeStruct((SAMPLE_ROWS, n), F32),
        compiler_params=_cp("arbitrary"),
        name="ada",
    )(c16, w, b.reshape(1, n))


def _norm_mod_kernel(x_ref, w_ref, sc_ref, sh_ref, o_ref):
    x = x_ref[0]
    y = x * lax.rsqrt(jnp.mean(x * x, axis=-1, keepdims=True) + EPS) * w_ref[...]
    o_ref[0] = (y * (1.0 + sc_ref[0]) + sh_ref[0]).astype(o_ref.dtype)


def _norm_mod(x3, w, scale3, shift3, tt):
    g, r, d = x3.shape
    rm = scale3.shape[1]
    if rm == 1:
        mod_spec = pl.BlockSpec((1, 1, d), lambda b, t: (b, 0, 0))
    else:
        mod_spec = pl.BlockSpec((1, tt, d), lambda b, t: (b, t, 0))
    return pl.pallas_call(
        _norm_mod_kernel,
        grid=(g, r // tt),
        in_specs=[pl.BlockSpec((1, tt, d), lambda b, t: (b, t, 0)),
                  pl.BlockSpec((1, d), lambda b, t: (0, 0)),
                  mod_spec, mod_spec],
        out_specs=pl.BlockSpec((1, tt, d), lambda b, t: (b, t, 0)),
        out_shape=jax.ShapeDtypeStruct((g, r, d), BF16),
        compiler_params=_cp("arbitrary", "arbitrary"),
        name="norm_mod",
    )(x3, w.reshape(1, d), scale3, shift3)


def _rms_kernel(x_ref, w_ref, o_ref):
    x = x_ref[...]
    o_ref[...] = x * lax.rsqrt(jnp.mean(x * x, axis=-1, keepdims=True) + EPS) * w_ref[...]


def _rmsnorm(x2, w, tt):
    r, d = x2.shape
    return pl.pallas_call(
        _rms_kernel,
        grid=(r // tt,),
        in_specs=[pl.BlockSpec((tt, d), lambda t: (t, 0)), pl.BlockSpec((1, d), lambda t: (0, 0))],
        out_specs=pl.BlockSpec((tt, d), lambda t: (t, 0)),
        out_shape=jax.ShapeDtypeStruct((r, d), F32),
        compiler_params=_cp("arbitrary"),
        name="rmsnorm",
    )(x2, w.reshape(1, d))


def _mm_kernel(x_ref, w_ref, o_ref):
    o_ref[...] = _dg(x_ref[...], w_ref[...])


def _matmul(x, w, tm, tn, name):
    m, k = x.shape
    n = w.shape[1]
    return pl.pallas_call(
        _mm_kernel,
        grid=(m // tm, n // tn),
        in_specs=[pl.BlockSpec((tm, k), lambda i, j: (i, 0)),
                  pl.BlockSpec((k, tn), lambda i, j: (0, j))],
        out_specs=pl.BlockSpec((tm, tn), lambda i, j: (i, j)),
        out_shape=jax.ShapeDtypeStruct((m, n), F32),
        compiler_params=_cp("arbitrary", "arbitrary"),
        name=name,
    )(x, w)


def _mm_out_kernel(a_ref, b_ref, c_ref, d_ref, w_ref, res_ref, gate_ref, o_ref):
    g = a_ref.shape[1]
    acc = _dg(a_ref[...], w_ref[0:g, :])
    acc = acc + _dg(b_ref[...], w_ref[g:2 * g, :])
    acc = acc + _dg(c_ref[...], w_ref[2 * g:3 * g, :])
    acc = acc + _dg(d_ref[...], w_ref[3 * g:4 * g, :])
    o_ref[...] = res_ref[...] + gate_ref[0] * acc


def _gate_spec(gate3, tm, tn, rows_per_group):
    if gate3.shape[1] == 1:
        return pl.BlockSpec((1, 1, tn), lambda i, j, *_: ((i * tm) // rows_per_group, 0, j))
    return pl.BlockSpec((1, tm, tn), lambda i, j, *_: (0, i, j))


def _mm_out(parts, w, res, gate3, tm, tn, rows_per_group):
    m, g = parts[0].shape
    n = w.shape[1]
    part_spec = pl.BlockSpec((tm, g), lambda i, j: (i, 0))
    return pl.pallas_call(
        _mm_out_kernel,
        grid=(m // tm, n // tn),
        in_specs=[part_spec, part_spec, part_spec, part_spec,
                  pl.BlockSpec((4 * g, tn), lambda i, j: (0, j)),
                  pl.BlockSpec((tm, tn), lambda i, j: (i, j)),
                  _gate_spec(gate3, tm, tn, rows_per_group)],
        out_specs=pl.BlockSpec((tm, tn), lambda i, j: (i, j)),
        out_shape=jax.ShapeDtypeStruct((m, n), F32),
        compiler_params=_cp("arbitrary", "arbitrary"),
        name="mm_out",
    )(*parts, w, res, gate3)


def _down_kernel(a_ref, w_ref, res_ref, gate_ref, o_ref):
    k = pl.program_id(2)
    part = _dg(a_ref[...], w_ref[...])

    @pl.when(k == 0)
    def _():
        o_ref[...] = part

    @pl.when(k > 0)
    def _():
        o_ref[...] += part

    @pl.when(k == pl.num_programs(2) - 1)
    def _():
        o_ref[...] = res_ref[...] + gate_ref[0] * o_ref[...]


def _mm_down(act, w, res, gate3, tm, tn, tk, rows_per_group):
    m, kk = act.shape
    n = w.shape[1]
    return pl.pallas_call(
        _down_kernel,
        grid=(m // tm, n // tn, kk // tk),
        in_specs=[pl.BlockSpec((tm, tk), lambda i, j, k: (i, k)),
                  pl.BlockSpec((tk, tn), lambda i, j, k: (k, j)),
                  pl.BlockSpec((tm, tn), lambda i, j, k: (i, j)),
                  _gate_spec(gate3, tm, tn, rows_per_group)],
        out_specs=pl.BlockSpec((tm, tn), lambda i, j, k: (i, j)),
        out_shape=jax.ShapeDtypeStruct((m, n), F32),
        compiler_params=_cp("arbitrary", "arbitrary", "arbitrary"),
        name="mm_down",
    )(act, w, res, gate3)


def _up_prompt_kernel(x_ref, wg_ref, wv_ref, cw_ref, act_ref, tail_ref, carry_ref, *, tiles_per_seq):
    i = pl.program_id(1)
    x = x_ref[...]
    g = _dg(x, wg_ref[...])
    v = _dg(x, wv_ref[...])
    tm = g.shape[0]

    @pl.when(i % tiles_per_seq == 0)
    def _():
        carry_ref[...] = jnp.zeros_like(carry_ref)

    prev = carry_ref[...]
    cw = cw_ref[...]
    hg = g * cw[2:3, :] + _shift_rows(g, 1) * cw[1:2, :] + _shift_rows(g, 2) * cw[0:1, :]
    row8 = _iota(prev.shape, 0)
    top = (jnp.where(row8 < 1, pltpu.roll(prev, 1, axis=0), 0.0) * cw[1:2, :]
           + jnp.where(row8 < 2, pltpu.roll(prev, 2, axis=0), 0.0) * cw[0:1, :])
    hg = jnp.concatenate([hg[0:8, :] + top, hg[8:, :]], axis=0)
    act_ref[...] = (_silu(hg) * v).astype(act_ref.dtype)
    carry_ref[...] = g[tm - 8:tm, :]
    tail_ref[0] = g[tm - 8:tm, :]


def _up_prompt(h2, w_up, conv_w, tm, tn, seq):
    m, k = h2.shape
    nf = w_up.shape[1] // 2
    nj = nf // tn
    kern = functools.partial(_up_prompt_kernel, tiles_per_seq=seq // tm)
    return pl.pallas_call(
        kern,
        grid=(nj, m // tm),
        in_specs=[pl.BlockSpec((tm, k), lambda j, i: (i, 0)),
                  pl.BlockSpec((k, tn), lambda j, i: (0, j)),
                  pl.BlockSpec((k, tn), lambda j, i: (0, nj + j)),
                  pl.BlockSpec((3, tn), lambda j, i: (0, j))],
        out_specs=[pl.BlockSpec((tm, tn), lambda j, i: (i, j)),
                   pl.BlockSpec((1, 8, tn), lambda j, i: ((i * tm) // seq, 0, j))],
        out_shape=[jax.ShapeDtypeStruct((m, nf), BF16),
                   jax.ShapeDtypeStruct((m // seq, 8, nf), F32)],
        scratch_shapes=[pltpu.VMEM((8, tn), F32)],
        compiler_params=_cp("arbitrary", "arbitrary"),
        name="up_prompt",
    )(h2, w_up, w_up, conv_w)


def _up_sample_kernel(x_ref, wg_ref, wv_ref, cw_ref, b0_ref, b1_ref, act_ref, g_ref):
    x = x_ref[...]
    g = _dg(x, wg_ref[...])
    v = _dg(x, wv_ref[...])
    cw = cw_ref[...]
    hg = b0_ref[...] * cw[0:1, :] + b1_ref[...] * cw[1:2, :] + g * cw[2:3, :]
    act_ref[...] = (_silu(hg) * v).astype(act_ref.dtype)
    g_ref[...] = g


def _up_sample(h2, w_up, conv_w, buf0, buf1, tn):
    m, k = h2.shape
    nf = w_up.shape[1] // 2
    nj = nf // tn
    row_spec = pl.BlockSpec((m, tn), lambda j: (0, j))
    return pl.pallas_call(
        _up_sample_kernel,
        grid=(nj,),
        in_specs=[pl.BlockSpec((m, k), lambda j: (0, 0)),
                  pl.BlockSpec((k, tn), lambda j: (0, j)),
                  pl.BlockSpec((k, tn), lambda j: (0, nj + j)),
                  pl.BlockSpec((3, tn), lambda j: (0, j)),
                  row_spec, row_spec],
        out_specs=[row_spec, row_spec],
        out_shape=[jax.ShapeDtypeStruct((m, nf), BF16), jax.ShapeDtypeStruct((m, nf), F32)],
        compiler_params=_cp("arbitrary"),
        name="up_sample",
    )(h2, w_up, w_up, conv_w, buf0, buf1)


def _head_scalars(ps, alog_row, dt_row, fb_row):
    lane = _iota(ps.shape, 1)
    beta = jax.nn.sigmoid(ps)
    g = -jnp.exp(alog_row) * jax.nn.softplus(ps + dt_row)
    logf = jax.nn.log_sigmoid(ps + fb_row)
    return jnp.where(lane < L_G, beta, jnp.where(lane < L_F, g, logf))


def _aux_kernel(ps_ref, alog_ref, dt_ref, fb_ref, a1_ref, a2_ref, ft_ref):
    t = ps_ref.shape[1]
    a1 = _head_scalars(ps_ref[0], alog_ref[...], dt_ref[...], fb_ref[...])
    a1_ref[0] = a1
    c = GDN_CHUNK
    tril = (_iota((c, c), 0) >= _iota((c, c), 1)).astype(BF16)
    keep = (_iota((1, HEAD_DIM), 1) >= L_F).astype(F32)
    carry = jnp.zeros((1, HEAD_DIM), F32)
    for ci in range(t // c):
        cs = _mm_exact_lhs(tril, a1[ci * c:(ci + 1) * c, :]) + carry * keep
        a2_ref[0, ci * c:(ci + 1) * c, :] = cs
        carry = cs[c - 1:c, :]
    a2t = a2_ref[0].T
    ft_ref[0] = a2t[L_F:L_F + N_HEADS, :]


def _aux(ps3, alog_row, dt_row, fb_row):
    b, t, n = ps3.shape
    row = pl.BlockSpec((1, n), lambda i: (0, 0))
    blk = pl.BlockSpec((1, t, n), lambda i: (i, 0, 0))
    return pl.pallas_call(
        _aux_kernel,
        grid=(b,),
        in_specs=[blk, row, row, row],
        out_specs=[blk, blk, pl.BlockSpec((1, N_HEADS, t), lambda i: (i, 0, 0))],
        out_shape=[jax.ShapeDtypeStruct((b, t, n), F32), jax.ShapeDtypeStruct((b, t, n), F32),
                   jax.ShapeDtypeStruct((b, N_HEADS, t), F32)],
        compiler_params=_cp("arbitrary"),
        name="aux",
    )(ps3, alog_row, dt_row, fb_row)


GDN_HG = 2


def _unit_lower_inverse(low):
    n = low.shape[0]
    nb = n // 8
    eye = (_iota((n, n), 0) == _iota((n, n), 1)).astype(F32)
    xb = [eye[8 * r:8 * r + 8, :] for r in range(nb)]
    lb = [low[8 * r:8 * r + 8, :] for r in range(nb)]
    for j in range(n - 1):
        rj = xb[j // 8][j % 8:j % 8 + 1, :]
        for r in range(j // 8, nb):
            xb[r] = xb[r] - lb[r][:, j:j + 1] * rj
    return jnp.concatenate(xb, axis=0)


def _gdn_prompt_kernel(q_ref, k_ref, v_ref, z_ref, wq_ref, wk_ref, wv_ref, a1_ref, a2_ref, nw_ref,
                       o_ref, s_ref, qs, ks, vs, us, ws, qks, egl):
    t = q_ref.shape[0]
    c = GDN_CHUNK
    nc = t // c
    hbase = pl.program_id(1) * GDN_HG

    for h in range(GDN_HG):
        sl = slice(h * HEAD_DIM, (h + 1) * HEAD_DIM)
        q = _silu(_causal_conv(q_ref[:, sl], wq_ref[:, sl]))
        k = _silu(_causal_conv(k_ref[:, sl], wk_ref[:, sl]))
        v = _silu(_causal_conv(v_ref[:, sl], wv_ref[:, sl]))
        q = q * lax.rsqrt(jnp.sum(q * q, axis=-1, keepdims=True) + EPS) * (HEAD_DIM ** -0.5)
        k = k * lax.rsqrt(jnp.sum(k * k, axis=-1, keepdims=True) + EPS)
        qs[h] = q
        ks[h] = k
        vs[h] = v

    ri = _iota((c, c), 0)
    ci_ = _iota((c, c), 1)
    causal = ri >= ci_
    strict = ri > ci_
    eye = ri == ci_

    def prep(ci, carry):
        rows = pl.ds(pl.multiple_of(ci * c, c), c)
        a1 = a1_ref[0, rows, :]
        a2 = a2_ref[0, rows, :]
        for h in range(GDN_HG):
            beta = _lane_pick(a1, L_BETA + hbase + h)
            gc = _lane_pick(a2, L_G + hbase + h)
            gc_row = jnp.sum(jnp.where(eye, gc, 0.0), axis=0, keepdims=True)
            diff = gc - gc_row
            decay = jnp.where(causal, jnp.exp(jnp.where(causal, diff, 0.0)), 0.0)
            q = qs[h, rows, :]
            k = ks[h, rows, :]
            v = vs[h, rows, :]
            kb = k * beta
            low = _mm3(kb, k, NT) * jnp.where(strict, decay, 0.0)
            inv = _unit_lower_inverse(low)
            eg = jnp.exp(gc)
            us[h, rows, :] = _mm3(inv, v * beta)
            ws[h, rows, :] = _mm3(inv, kb * eg)
            qks[h, rows, :] = _mm3(q, k, NT) * decay
            g_last = gc[c - 1:c, :]
            qs[h, rows, :] = q * eg
            ks[h, rows, :] = k * jnp.exp(g_last - gc)
            egl[h, pl.ds(pl.multiple_of(ci * 8, 8), 8), :] = jnp.broadcast_to(jnp.exp(g_last), (8, HEAD_DIM))
        return carry

    lax.fori_loop(0, nc, prep, 0)

    nw = nw_ref[...]

    def sweep(ci, states):
        rows = pl.ds(pl.multiple_of(ci * c, c), c)
        new_states = []
        for h in range(GDN_HG):
            s = states[h]
            v_new = us[h, rows, :] - _mm3(ws[h, rows, :], s)
            o = _mm3(qs[h, rows, :], s) + _mm3(qks[h, rows, :], v_new)
            e_last = egl[h, pl.ds(pl.multiple_of(ci * 8, 8), 8), :][0:1, :]
            new_states.append(s * e_last + _mm3(ks[h, rows, :], v_new, TN))
            o = o * lax.rsqrt(jnp.mean(o * o, axis=-1, keepdims=True) + EPS) * nw
            z = z_ref[rows, h * HEAD_DIM:(h + 1) * HEAD_DIM]
            o_ref[rows, h * HEAD_DIM:(h + 1) * HEAD_DIM] = (o * _silu(z)).astype(o_ref.dtype)
        return tuple(new_states)

    init = tuple(jnp.zeros((HEAD_DIM, HEAD_DIM), F32) for _ in range(GDN_HG))
    final = lax.fori_loop(0, nc, sweep, init)
    for h in range(GDN_HG):
        s_ref[0, h] = final[h]


def _gdn_prompt(proj, a1, a2, conv_w, norm_w, b, t):
    hw = GDN_HG * HEAD_DIM
    ng = W_GROUP // hw
    col = lambda off: pl.BlockSpec((t, hw), lambda i, j: (i, off * ng + j))
    wcol = lambda off: pl.BlockSpec((4, hw), lambda i, j: (0, off * ng + j))
    aux = pl.BlockSpec((1, t, HEAD_DIM), lambda i, j: (i, 0, 0))
    scr = lambda n: pltpu.VMEM((GDN_HG, t, n), F32)
    return pl.pallas_call(
        _gdn_prompt_kernel,
        grid=(b, ng),
        in_specs=[col(0), col(1), col(2), col(3), wcol(0), wcol(1), wcol(2), aux, aux,
                  pl.BlockSpec((1, HEAD_DIM), lambda i, j: (0, 0))],
        out_specs=[pl.BlockSpec((t, hw), lambda i, j: (i, j)),
                   pl.BlockSpec((1, GDN_HG, HEAD_DIM, HEAD_DIM), lambda i, j: (i, j, 0, 0))],
        out_shape=[jax.ShapeDtypeStruct((b * t, W_GROUP), BF16),
                   jax.ShapeDtypeStruct((b, N_HEADS, HEAD_DIM, HEAD_DIM), F32)],
        scratch_shapes=[scr(HEAD_DIM), scr(HEAD_DIM), scr(HEAD_DIM), scr(HEAD_DIM), scr(HEAD_DIM),
                        scr(GDN_CHUNK), pltpu.VMEM((GDN_HG, (t // GDN_CHUNK) * 8, HEAD_DIM), F32)],
        compiler_params=_cp("arbitrary", "arbitrary"),
        name="gdn_prompt",
    )(proj, proj, proj, proj, conv_w, conv_w, conv_w, a1, a2, norm_w.reshape(1, HEAD_DIM))


def _sconv_prompt_kernel(x_ref, gb_ref, gc_ref, w_ref, o_ref, tail_ref):
    u = gc_ref[...] * x_ref[...]
    t = u.shape[0]
    o_ref[...] = (gb_ref[...] * _causal_conv(u, w_ref[...])).astype(o_ref.dtype)
    tail_ref[0] = u[t - 8:t, :]


def _sconv_prompt(proj, w, b, t):
    tc = 256
    nb = W_GROUP // tc
    base = M_B // tc
    col = lambda k: pl.BlockSpec((t, tc), lambda i, j: (i, base + k * nb + j))
    return pl.pallas_call(
        _sconv_prompt_kernel,
        grid=(b, nb),
        in_specs=[col(0), col(1), col(2), pl.BlockSpec((3, tc), lambda i, j: (0, j))],
        out_specs=[pl.BlockSpec((t, tc), lambda i, j: (i, j)),
                   pl.BlockSpec((1, 8, tc), lambda i, j: (i, 0, j))],
        out_shape=[jax.ShapeDtypeStruct((b * t, W_GROUP), BF16),
                   jax.ShapeDtypeStruct((b, 8, W_GROUP), F32)],
        compiler_params=_cp("arbitrary", "arbitrary"),
        name="sconv_prompt",
    )(proj, proj, proj, w)


def _fox_prompt_kernel(q_ref, k_ref, v_ref, a2_ref, ft_ref, o_ref):
    h = pl.program_id(1)
    qi = pl.program_id(2)
    q = q_ref[...]
    tq = q.shape[0]
    t = k_ref.shape[0]
    s = _mm1(q, k_ref[...], NT) * (HEAD_DIM ** -0.5)
    fq = _lane_pick(a2_ref[0], L_F + h)
    ft = ft_ref[0]
    fk = jnp.sum(jnp.where(_iota(ft.shape, 0) == h, ft, 0.0), axis=0, keepdims=True)
    s = s + fq - fk
    qpos = qi * tq + _iota((tq, t), 0)
    s = jnp.where(_iota((tq, t), 1) <= qpos, s, -jnp.inf)
    m = jnp.max(s, axis=-1, keepdims=True)
    p = jnp.exp(s - m)
    l = jnp.sum(p, axis=-1, keepdims=True)
    o_ref[...] = (_mm1(p, v_ref[...]) / l).astype(o_ref.dtype)


def _fox_prompt(proj, a2, ft, b, t):
    tq = 256
    nq = t // tq
    base = M_C // HEAD_DIM
    return pl.pallas_call(
        _fox_prompt_kernel,
        grid=(b, N_HEADS, nq),
        in_specs=[pl.BlockSpec((tq, HEAD_DIM), lambda i, h, q: (i * nq + q, base + h)),
                  pl.BlockSpec((t, HEAD_DIM), lambda i, h, q: (i, base + N_HEADS + h)),
                  pl.BlockSpec((t, HEAD_DIM), lambda i, h, q: (i, base + 2 * N_HEADS + h)),
                  pl.BlockSpec((1, tq, HEAD_DIM), lambda i, h, q: (i, q, 0)),
                  pl.BlockSpec((1, N_HEADS, t), lambda i, h, q: (i, 0, 0))],
        out_specs=pl.BlockSpec((tq, HEAD_DIM), lambda i, h, q: (i * nq + q, h)),
        out_shape=jax.ShapeDtypeStruct((b * t, W_GROUP), BF16),
        compiler_params=_cp("arbitrary", "arbitrary", "arbitrary"),
        name="fox_prompt",
    )(proj, proj, proj, a2, ft)


def _lru_gates(xd, wr, br, wi, bi, lam):
    r = jax.nn.sigmoid(_mm3(xd, wr) + br)
    i = jax.nn.sigmoid(_mm3(xd, wi) + bi)
    log_a = -LRU_C * r * jax.nn.softplus(-lam)
    a = jnp.exp(log_a)
    th = jnp.tanh(log_a)
    one_minus_a2 = -2.0 * th / (1.0 - th)
    return a, jnp.sqrt(one_minus_a2) * (i * xd)


def _lru_prompt_kernel(x_ref, y_ref, cw_ref, cb_ref, wr_ref, br_ref, wi_ref, bi_ref, lam_ref,
                       o_ref, tail_ref):
    t = x_ref.shape[0]
    xd = _causal_conv(x_ref[...], cw_ref[...]) + cb_ref[...]
    a, bb = _lru_gates(xd, wr_ref[0], br_ref[...], wi_ref[0], bi_ref[...], lam_ref[...])
    row = _iota(a.shape, 0)
    s = 1
    while s < t:
        a_s = jnp.where(row < s, 1.0, pltpu.roll(a, s, axis=0))
        b_s = jnp.where(row < s, 0.0, pltpu.roll(bb, s, axis=0))
        bb = a * b_s + bb
        a = a * a_s
        s *= 2
    o_ref[...] = (jax.nn.gelu(y_ref[...]) * bb).astype(o_ref.dtype)
    tail_ref[0] = bb[t - 8:t, :]


def _lru_prompt(proj, cw, cb, wr, br, wi, bi, lam, b, t):
    n = HEAD_DIM
    nb = W_GROUP // n
    base = M_D // n
    vec = lambda: pl.BlockSpec((1, n), lambda i, j: (0, j))
    mat = lambda: pl.BlockSpec((1, n, n), lambda i, j: (j, 0, 0))
    r1 = lambda a: a.reshape(1, W_GROUP)
    return pl.pallas_call(
        _lru_prompt_kernel,
        grid=(b, nb),
        in_specs=[pl.BlockSpec((t, n), lambda i, j: (i, base + j)),
                  pl.BlockSpec((t, n), lambda i, j: (i, base + nb + j)),
                  pl.BlockSpec((4, n), lambda i, j: (0, j)),
                  vec(), mat(), vec(), mat(), vec(), vec()],
        out_specs=[pl.BlockSpec((t, n), lambda i, j: (i, j)),
                   pl.BlockSpec((1, 8, n), lambda i, j: (i, 0, j))],
        out_shape=[jax.ShapeDtypeStruct((b * t, W_GROUP), BF16),
                   jax.ShapeDtypeStruct((b, 8, W_GROUP), F32)],
        compiler_params=_cp("arbitrary", "arbitrary"),
        name="lru_prompt",
    )(proj, proj, cw, r1(cb), wr, r1(br), wi, r1(bi), r1(lam))


def _gdn_sample_kernel(q_ref, k_ref, v_ref, z_ref, b0_ref, b1_ref, b2_ref, cw_ref, ps_ref,
                       alog_ref, dt_ref, fb_ref, nw_ref, s_ref, o_ref, so_ref, hs_ref):
    i = pl.program_id(0)
    cw = cw_ref[...]
    hs = _head_scalars(ps_ref[...], alog_ref[...], dt_ref[...], fb_ref[...])
    hs_ref[...] = hs
    hs_i = jnp.sum(jnp.where(_iota(hs.shape, 0) == i, hs, 0.0), axis=0, keepdims=True)
    row_i = lambda ref: ref[pl.ds(i, 1), :]
    b0 = row_i(b0_ref)
    b1 = row_i(b1_ref)
    b2 = row_i(b2_ref)
    eye = _iota((HEAD_DIM, HEAD_DIM), 0) == _iota((HEAD_DIM, HEAD_DIM), 1)
    nw = nw_ref[...]

    def conv(x_row, off):
        w = cw[:, off:off + W_GROUP]
        bo = slice(off, off + W_GROUP)
        return _silu(b0[:, bo] * w[0:1, :] + b1[:, bo] * w[1:2, :] + b2[:, bo] * w[2:3, :] + x_row * w[3:4, :])

    q_all = conv(row_i(q_ref), 0)
    k_all = conv(row_i(k_ref), W_GROUP)
    v_all = conv(row_i(v_ref), 2 * W_GROUP)
    z_all = row_i(z_ref)
    for h in range(N_HEADS):
        sl = slice(h * HEAD_DIM, (h + 1) * HEAD_DIM)
        q = q_all[:, sl]
        k = k_all[:, sl]
        v = v_all[:, sl]
        q = q * lax.rsqrt(jnp.sum(q * q, axis=-1, keepdims=True) + EPS) * (HEAD_DIM ** -0.5)
        k = k * lax.rsqrt(jnp.sum(k * k, axis=-1, keepdims=True) + EPS)
        beta = hs_i[:, L_BETA + h:L_BETA + h + 1]
        g = hs_i[:, L_G + h:L_G + h + 1]
        k_col = jnp.sum(jnp.where(eye, k, 0.0), axis=1, keepdims=True)
        q_col = jnp.sum(jnp.where(eye, q, 0.0), axis=1, keepdims=True)
        s = s_ref[0, h] * jnp.exp(g)
        delta = (v - jnp.sum(k_col * s, axis=0, keepdims=True)) * beta
        s = s + k_col * delta
        so_ref[0, h] = s
        o = jnp.sum(q_col * s, axis=0, keepdims=True)
        o = o * lax.rsqrt(jnp.mean(o * o, axis=-1, keepdims=True) + EPS) * nw
        o_ref[0, :, sl] = o * _silu(z_all[:, sl])


def _gdn_sample(proj, ps, buf, conv_w, alog_row, dt_row, fb_row, norm_w, state, bs):
    rows = proj.shape[0]
    col = lambda k: pl.BlockSpec((rows, W_GROUP), lambda i: (0, k))
    full = lambda a: pl.BlockSpec(a.shape, lambda i: (0,) * a.ndim)
    prow = pl.BlockSpec((1, HEAD_DIM), lambda i: (0, 0))
    sblk = pl.BlockSpec((1, N_HEADS, HEAD_DIM, HEAD_DIM), lambda i: (i, 0, 0, 0))
    return pl.pallas_call(
        _gdn_sample_kernel,
        grid=(bs,),
        in_specs=[col(0), col(1), col(2), col(3), full(buf[0]), full(buf[1]), full(buf[2]),
                  full(conv_w), pl.BlockSpec((rows, HEAD_DIM), lambda i: (0, 0)), prow, prow, prow, prow, sblk],
        out_specs=[pl.BlockSpec((1, 1, W_GROUP), lambda i: (i, 0, 0)), sblk,
                   pl.BlockSpec((rows, HEAD_DIM), lambda i: (0, 0))],
        out_shape=[jax.ShapeDtypeStruct((bs, 1, W_GROUP), F32),
                   jax.ShapeDtypeStruct(state.shape, F32),
                   jax.ShapeDtypeStruct((rows, HEAD_DIM), F32)],
        compiler_params=_cp("arbitrary"),
        name="gdn_sample",
    )(proj, proj, proj, proj, buf[0], buf[1], buf[2], conv_w, ps, alog_row, dt_row, fb_row,
      norm_w.reshape(1, HEAD_DIM), state)


def _bd_sample_kernel(xb_ref, gb_ref, gc_ref, sb0_ref, sb1_ref, sw_ref,
                      xd_ref, yd_ref, lb0_ref, lb1_ref, lb2_ref, lh_ref, cw_ref, cb_ref,
                      wr_ref, br_ref, wi_ref, bi_ref, lam_ref,
                      ob_ref, u_ref, od_ref, h_ref):
    sw = sw_ref[...]
    u = gc_ref[...] * xb_ref[...]
    ob_ref[...] = gb_ref[...] * (sb0_ref[...] * sw[0:1, :] + sb1_ref[...] * sw[1:2, :] + u * sw[2:3, :])
    u_ref[...] = u
    cw = cw_ref[...]
    xd = (lb0_ref[...] * cw[0:1, :] + lb1_ref[...] * cw[1:2, :] + lb2_ref[...] * cw[2:3, :]
          + xd_ref[...] * cw[3:4, :] + cb_ref[...])
    n = HEAD_DIM
    for j in range(W_GROUP // n):
        sl = slice(j * n, (j + 1) * n)
        a, bb = _lru_gates(xd[:, sl], wr_ref[j], br_ref[:, sl], wi_ref[j], bi_ref[:, sl], lam_ref[:, sl])
        hn = a * lh_ref[:, sl] + bb
        h_ref[:, sl] = hn
        od_ref[:, sl] = jax.nn.gelu(yd_ref[:, sl]) * hn


def _bd_sample(proj, sbuf, sw, lbuf, lh, cw, cb, wr, br, wi, bi, lam):
    rows = proj.shape[0]
    r1 = lambda a: a.reshape(1, W_GROUP)
    pcol = lambda c: pl.BlockSpec((rows, W_GROUP), lambda i: (0, c // W_GROUP))
    full = lambda a: pl.BlockSpec(a.shape, lambda i: (0,) * a.ndim)
    ins = [proj, proj, proj, sbuf[0], sbuf[1], sw, proj, proj, lbuf[0], lbuf[1], lbuf[2], lh, cw,
           r1(cb), wr, r1(br), wi, r1(bi), r1(lam)]
    specs = [pcol(M_B), pcol(M_B + W_GROUP), pcol(M_B + 2 * W_GROUP)] + [full(a) for a in ins[3:6]] \
        + [pcol(M_D), pcol(M_D + W_GROUP)] + [full(a) for a in ins[8:]]
    out = jax.ShapeDtypeStruct((rows, W_GROUP), F32)
    ospec = pl.BlockSpec((rows, W_GROUP), lambda i: (0, 0))
    return pl.pallas_call(
        _bd_sample_kernel,
        grid=(1,),
        in_specs=specs,
        out_specs=[ospec] * 4,
        out_shape=[out] * 4,
        compiler_params=_cp("arbitrary"),
        name="bd_sample",
    )(*ins)


def _fox_sample_kernel(pt_ref, q_ref, kn_ref, vn_ref, hs_ref, kp_ref, vp_ref, lf_ref,
                       o_ref, m_sc, l_sc, acc_sc, r_sc):
    del pt_ref
    i = pl.program_id(0)
    p = pl.program_id(1)
    npg = pl.num_programs(1)
    w = W_GROUP
    scale = HEAD_DIM ** -0.5
    e = (_iota((w, N_HEADS), 0) // HEAD_DIM == _iota((w, N_HEADS), 1)).astype(BF16)
    et = (_iota((N_HEADS, w), 1) // HEAD_DIM == _iota((N_HEADS, w), 0)).astype(BF16)
    q = q_ref[pl.ds(i, 1), :]

    @pl.when(p == 0)
    def _():
        kn = kn_ref[pl.ds(i, 1), :]
        s_new = _mm_exact_rhs(q * kn, e) * scale
        m_sc[...] = s_new
        l_sc[...] = jnp.ones_like(l_sc)
        acc_sc[...] = vn_ref[pl.ds(i, 1), :]
        hs = hs_ref[...]
        hs_i = jnp.sum(jnp.where(_iota(hs.shape, 0) == i, hs, 0.0), axis=0, keepdims=True)
        r_sc[...] = hs_i[:, L_F:L_F + N_HEADS]

    kp = kp_ref[...]
    vp = vp_ref[...]
    lf = lf_ref[...]
    pg = kp.shape[0]
    upper = (_iota((pg, pg), 1) > _iota((pg, pg), 0)).astype(BF16)
    suffix = _mm_exact_lhs(upper, lf)
    s = _mm_exact_rhs(kp * q, e) * scale + suffix + r_sc[...]
    m_old = m_sc[...]
    m_new = jnp.maximum(m_old, jnp.max(s, axis=0, keepdims=True))
    alpha = jnp.exp(m_old - m_new)
    pw = jnp.exp(s - m_new)
    l_sc[...] = alpha * l_sc[...] + jnp.sum(pw, axis=0, keepdims=True)
    m_sc[...] = m_new
    pw_full = _mm_exact_rhs(pw, et)
    acc_sc[...] = _mm_exact_rhs(alpha, et) * acc_sc[...] + jnp.sum(pw_full * vp, axis=0, keepdims=True)
    r_sc[...] = r_sc[...] + jnp.sum(lf, axis=0, keepdims=True)

    @pl.when(p == npg - 1)
    def _():
        o_ref[0] = acc_sc[...] / _mm_exact_rhs(l_sc[...], et)


def _fox_sample(proj, hs, page_table, ck, cv, clf, layer, bs):
    rows = proj.shape[0]
    n_pages = page_table.shape[1]
    page = ck.shape[2]
    ck2 = ck.reshape(ck.shape[0], ck.shape[1], page, W_GROUP)
    cv2 = cv.reshape(cv.shape[0], cv.shape[1], page, W_GROUP)
    base = M_C // W_GROUP
    col = lambda k: pl.BlockSpec((rows, W_GROUP), lambda i, p, pt: (0, base + k))
    pidx = lambda i, p, pt: (layer, pt[i, n_pages - 1 - p], 0, 0)
    grid_spec = pltpu.PrefetchScalarGridSpec(
        num_scalar_prefetch=1,
        grid=(bs, n_pages),
        in_specs=[col(0), col(1), col(2),
                  pl.BlockSpec((rows, HEAD_DIM), lambda i, p, pt: (0, 0)),
                  pl.BlockSpec((None, None, page, W_GROUP), pidx),
                  pl.BlockSpec((None, None, page, W_GROUP), pidx),
                  pl.BlockSpec((None, None, page, N_HEADS), pidx)],
        out_specs=pl.BlockSpec((1, 1, W_GROUP), lambda i, p, pt: (i, 0, 0)),
        scratch_shapes=[pltpu.VMEM((1, N_HEADS), F32), pltpu.VMEM((1, N_HEADS), F32),
                        pltpu.VMEM((1, W_GROUP), F32), pltpu.VMEM((1, N_HEADS), F32)],
    )
    return pl.pallas_call(
        _fox_sample_kernel,
        grid_spec=grid_spec,
        out_shape=jax.ShapeDtypeStruct((bs, 1, W_GROUP), F32),
        compiler_params=_cp("arbitrary", "arbitrary"),
        name="fox_sample",
    )(page_table, proj, proj, proj, hs, ck2, cv2, clf)


def _prep_layer(w_in, w_out, w_up, w_down, ffn_conv_w, a_log, dt_bias, f_bias):
    w_main = jnp.concatenate([w_in[:, 0:OFF_A_BETA], w_in[:, OFF_B:OFF_C_F], w_in[:, OFF_D:N_IN]],
                             axis=1).astype(BF16)
    w_small = jnp.concatenate([w_in[:, OFF_A_BETA:OFF_B], w_in[:, OFF_C_F:OFF_D],
                               jnp.zeros((w_in.shape[0], HEAD_DIM - 3 * N_HEADS), w_in.dtype)],
                              axis=1).astype(BF16)
    padc = ((0, 0), (0, D_FF_PAD - D_FF))
    w_up_p = jnp.concatenate([jnp.pad(w_up[:, :D_FF], padc), jnp.pad(w_up[:, D_FF:], padc)],
                             axis=1).astype(BF16)
    w_down_p = jnp.pad(w_down, ((0, D_FF_PAD - D_FF), (0, 0))).astype(BF16)
    lane_row = lambda v, off: jnp.zeros((1, HEAD_DIM), F32).at[0, off:off + N_HEADS].set(v)
    return dict(w_main=w_main, w_small=w_small, w_out=w_out.astype(BF16), w_up=w_up_p, w_down=w_down_p,
                ffn_cw=jnp.pad(ffn_conv_w, padc), alog_row=lane_row(a_log, L_G),
                dt_row=lane_row(dt_bias, L_G), fb_row=lane_row(f_bias, L_F))


def _pad_rows(a, rows):
    return jnp.pad(a, ((0, rows - a.shape[0]),) + ((0, 0),) * (a.ndim - 1))


def kernel(x_prompt, x_sample, cache_k, cache_v, cache_logf, state_gdn, state_gdn_conv, state_sconv,
           state_lru_h, state_lru_conv, state_ffn_conv, page_table, c_prompt, c_sample, ada_w, ada_b,
           ln1_w, ln2_w, w_in, gdn_conv_w, gdn_a_log, gdn_dt_bias, gdn_norm_w, sconv_w, fox_f_bias,
           lru_conv_w, lru_conv_b, lru_wr, lru_br, lru_wi, lru_bi, lru_lambda, w_out, ffn_conv_w,
           w_up, w_down, final_norm_w):
    bp, t, d = x_prompt.shape
    bs = x_sample.shape[0]
    depth = w_in.shape[0]
    sr = SAMPLE_ROWS
    tm = min(1024, t)
    tnorm = min(512, t)
    c16 =_pad_rows(jnp.concatenate([c_prompt, c_sample], axis=0), sr)

    xp = x_prompt.reshape(bp * t, d)
    xs = _pad_rows(x_sample.reshape(bs, d), sr)
    p_states, s_states = [], []
    for l in range(depth):
        w = _prep_layer(w_in[l], w_out[l], w_up[l], w_down[l], ffn_conv_w[l], gdn_a_log[l],
                        gdn_dt_bias[l], fox_f_bias[l])
        mod = _ada(c16, ada_w[l], ada_b[l]).reshape(sr, 6, d)
        pm = [mod[0:bp, k][:, None, :] for k in range(6)]
        sm = [_pad_rows(mod[bp:bp + bs, k], sr)[None] for k in range(6)]

        h = _norm_mod(xp.reshape(bp, t, d), ln1_w[l], pm[1], pm[0], tnorm).reshape(bp * t, d)
        proj = _matmul(h, w["w_main"], tm, 1024, "mm_in")
        ps = _matmul(h, w["w_small"], tm, HEAD_DIM, "mm_in_small")
        a1, a2, ft = _aux(ps.reshape(bp, t, HEAD_DIM), w["alog_row"], w["dt_row"], w["fb_row"])
        oa, p_gdn = _gdn_prompt(proj, a1, a2, gdn_conv_w[l], gdn_norm_w[l], bp, t)
        ob, sconv_tail = _sconv_prompt(proj, sconv_w[l], bp, t)
        oc = _fox_prompt(proj, a2, ft, bp, t)
        od, lru_tail = _lru_prompt(proj, lru_conv_w[l], lru_conv_b[l], lru_wr[l], lru_br[l], lru_wi[l],
                                   lru_bi[l], lru_lambda[l], bp, t)
        xp = _mm_out([oa, ob, oc, od], w["w_out"], xp, pm[2], tm, 1024, t)
        h2 = _norm_mod(xp.reshape(bp, t, d), ln2_w[l], pm[4], pm[3], tnorm).reshape(bp * t, d)
        act, ffn_tail = _up_prompt(h2, w["w_up"], w["ffn_cw"], tm, 512, t)
        xp = _mm_down(act, w["w_down"], xp, pm[5], tm, 1024, D_FF_PAD // 4, t)

        proj3 = proj.reshape(bp, t, N_MAIN)
        p_states.append((
            proj3[:, :, M_C + W_GROUP:M_C + 2 * W_GROUP].reshape(bp, t, N_HEADS, HEAD_DIM),
            proj3[:, :, M_C + 2 * W_GROUP:M_C + 3 * W_GROUP].reshape(bp, t, N_HEADS, HEAD_DIM),
            a1[:, :, L_F:L_F + N_HEADS],
            p_gdn,
            proj3[:, t - 3:, M_A:M_A + 3 * W_GROUP],
            sconv_tail[:, 6:8, :],
            lru_tail[:, 7, :],
            proj3[:, t - 3:, M_D:M_D + W_GROUP],
            ffn_tail[:, 6:8, :D_FF],
        ))

        hs_in = _norm_mod(xs[None], ln1_w[l], sm[1], sm[0], sr)[0]
        sproj = _matmul(hs_in, w["w_main"], sr, 2048, "mm_in_s")
        sps = _matmul(hs_in, w["w_small"], sr, HEAD_DIM, "mm_in_small_s")
        gbuf = [_pad_rows(state_gdn_conv[l][:, j, :], sr) for j in range(3)]
        soa, s_gdn, shs = _gdn_sample(sproj, sps, gbuf, gdn_conv_w[l], w["alog_row"], w["dt_row"],
                                      w["fb_row"], gdn_norm_w[l], state_gdn[l], bs)
        sbuf = [_pad_rows(state_sconv[l][:, j, :], sr) for j in range(2)]
        lbuf = [_pad_rows(state_lru_conv[l][:, j, :], sr) for j in range(3)]
        sob, s_u, sod, s_h = _bd_sample(sproj, sbuf, sconv_w[l], lbuf, _pad_rows(state_lru_h[l], sr),
                                        lru_conv_w[l], lru_conv_b[l], lru_wr[l], lru_br[l], lru_wi[l],
                                        lru_bi[l], lru_lambda[l])
        soc = _fox_sample(sproj, shs, page_table, cache_k, cache_v, cache_logf, l, bs)
        parts = [_pad_rows(soa.reshape(bs, W_GROUP), sr).astype(BF16), sob.astype(BF16),
                 _pad_rows(soc.reshape(bs, W_GROUP), sr).astype(BF16), sod.astype(BF16)]
        xs = _mm_out(parts, w["w_out"], xs, sm[2], sr, 2048, sr)
        sh2 = _norm_mod(xs[None], ln2_w[l], sm[4], sm[3], sr)[0]
        fbuf = [_pad_rows(jnp.pad(state_ffn_conv[l][:, j, :], ((0, 0), (0, D_FF_PAD - D_FF))), sr)
                for j in range(2)]
        sact, s_g = _up_sample(sh2, w["w_up"], w["ffn_cw"], fbuf[0], fbuf[1], 1024)
        xs = _mm_down(sact, w["w_down"], xs, sm[5], sr, 2048, D_FF_PAD // 4, sr)

        sp = sproj[:bs]
        s_states.append((
            sp[:, M_C + W_GROUP:M_C + 2 * W_GROUP].reshape(bs, 1, N_HEADS, HEAD_DIM),
            sp[:, M_C + 2 * W_GROUP:M_C + 3 * W_GROUP].reshape(bs, 1, N_HEADS, HEAD_DIM),
            shs[:bs, L_F:L_F + N_HEADS].reshape(bs, 1, N_HEADS),
            s_gdn,
            jnp.concatenate([state_gdn_conv[l][:, 1:, :], sp[:, None, M_A:M_A + 3 * W_GROUP]], axis=1),
            jnp.concatenate([state_sconv[l][:, 1:, :], s_u[:bs, None, :]], axis=1),
            s_h[:bs],
            jnp.concatenate([state_lru_conv[l][:, 1:, :], sp[:, None, M_D:M_D + W_GROUP]], axis=1),
            jnp.concatenate([state_ffn_conv[l][:, 1:, :], s_g[:bs, None, :D_FF]], axis=1),
        ))

    y_prompt = _rmsnorm(xp, final_norm_w, tnorm).reshape(bp, t, d)
    y_sample = _rmsnorm(xs, final_norm_w, sr)[:bs].reshape(bs, 1, d)
    p_out = tuple(jnp.stack(parts) for parts in zip(*p_states))
    s_out = tuple(jnp.stack(parts) for parts in zip(*s_states))
    return (y_prompt, y_sample) + p_out + s_out
```

```python
import functools

import jax
import jax.numpy as jnp
from jax import lax
from jax.experimental import pallas as pl
from jax.experimental.pallas import tpu as pltpu

F32 = jnp.float32
BF16 = jnp.bfloat16

D_MODEL = 4096
HEAD_DIM = 128
W_GROUP = D_MODEL // 4
N_HEADS = W_GROUP // HEAD_DIM
GDN_CHUNK = 64
D_FF = 11008
D_FF_PAD = 11264
LRU_C = 8.0
EPS = 1e-6
SAMPLE_ROWS = 16

OFF_A_BETA = 4 * W_GROUP
OFF_B = OFF_A_BETA + 2 * N_HEADS
OFF_C_F = OFF_B + 6 * W_GROUP
OFF_D = OFF_C_F + N_HEADS
N_IN = OFF_D + 2 * W_GROUP
M_A = 0
M_B = 4 * W_GROUP
M_C = 7 * W_GROUP
M_D = 10 * W_GROUP
N_MAIN = 12 * W_GROUP
L_BETA, L_G, L_F = 0, N_HEADS, 2 * N_HEADS

VMEM_LIMIT = 56 * 1024 * 1024

NN = ((1,), (0,))
NT = ((1,), (1,))
TN = ((0,), (0,))


def _cp(*sem):
    return pltpu.CompilerParams(dimension_semantics=sem, vmem_limit_bytes=VMEM_LIMIT)


def _dg(a, b, dims=NN):
    return lax.dot_general(a, b, (dims, ((), ())), preferred_element_type=F32)


def _mm1(a, b, dims=NN):
    return _dg(a.astype(BF16), b.astype(BF16), dims)


def _split(a):
    hi = a.astype(BF16)
    lo = (a - hi.astype(F32)).astype(BF16)
    return hi, lo


def _mm3(a, b, dims=NN):
    ah, al = _split(a)
    bh, bl = _split(b)
    return _dg(ah, bh, dims) + (_dg(ah, bl, dims) + _dg(al, bh, dims))


def _mm_exact_lhs(a_bf16, b):
    b0 = b.astype(BF16)
    r1 = b - b0.astype(F32)
    b1 = r1.astype(BF16)
    b2 = (r1 - b1.astype(F32)).astype(BF16)
    return _dg(a_bf16, b0) + (_dg(a_bf16, b1) + _dg(a_bf16, b2))


def _mm_exact_rhs(a, b_bf16):
    a0 = a.astype(BF16)
    r1 = a - a0.astype(F32)
    a1 = r1.astype(BF16)
    a2 = (r1 - a1.astype(F32)).astype(BF16)
    return _dg(a0, b_bf16) + (_dg(a1, b_bf16) + _dg(a2, b_bf16))


def _silu(x):
    return x * jax.nn.sigmoid(x)


def _iota(shape, axis):
    return lax.broadcasted_iota(jnp.int32, shape, axis)


def _shift_rows(x, s):
    r = pltpu.roll(x, s, axis=0)
    return jnp.where(_iota(x.shape, 0) < s, 0.0, r)


def _lane_pick(x, lane):
    return jnp.sum(jnp.where(_iota(x.shape, 1) == lane, x, 0.0), axis=-1, keepdims=True)


def _causal_conv(x, w):
    width = w.shape[0]
    y = x * w[width - 1:width, :]
    for s in range(1, width):
        y = y + _shift_rows(x, s) * w[width - 1 - s:width - s, :]
    return y


def _ada_kernel(c_ref, w_ref, b_ref, o_ref):
    c = c_ref[...]
    o_ref[...] = _mm1(_silu(c), w_ref[...]) + b_ref[...]


def _ada(c16, w, b, layer):
    _, k, n = w.shape
    tn = 512
    return pl.pallas_call(
        _ada_kernel,
        grid=(n // tn,),
        in_specs=[pl.BlockSpec((SAMPLE_ROWS, k), lambda j: (0, 0)),
                  pl.BlockSpec((None, k, tn), lambda j: (layer, 0, j)),
                  pl.BlockSpec((None, 1, tn), lambda j: (layer, 0, j))],
        out_specs=pl.BlockSpec((SAMPLE_ROWS, tn), lambda j: (0, j)),
        out_shape=jax.ShapeDtypeStruct((SAMPLE_ROWS, n), F32),
        compiler_params=_cp("arbitrary"),
        name="ada",
    )(c16, w, b.reshape(b.shape[0], 1, n))


def _norm_mod_kernel(x_ref, w_ref, sc_ref, sh_ref, o_ref):
    x = x_ref[0]
    y = x * lax.rsqrt(jnp.mean(x * x, axis=-1, keepdims=True) + EPS) * w_ref[...]
    o_ref[0] = (y * (1.0 + sc_ref[0]) + sh_ref[0]).astype(o_ref.dtype)


def _norm_mod(x3, w, scale3, shift3, tt):
    g, r, d = x3.shape
    rm = scale3.shape[1]
    if rm == 1:
        mod_spec = pl.BlockSpec((1, 1, d), lambda b, t: (b, 0, 0))
    else:
        mod_spec = pl.BlockSpec((1, tt, d), lambda b, t: (b, t, 0))
    return pl.pallas_call(
        _norm_mod_kernel,
        grid=(g, r // tt),
        in_specs=[pl.BlockSpec((1, tt, d), lambda b, t: (b, t, 0)),
                  pl.BlockSpec((1, d), lambda b, t: (0, 0)),
                  mod_spec, mod_spec],
        out_specs=pl.BlockSpec((1, tt, d), lambda b, t: (b, t, 0)),
        out_shape=jax.ShapeDtypeStruct((g, r, d), BF16),
        compiler_params=_cp("arbitrary", "arbitrary"),
        name="norm_mod",
    )(x3, w.reshape(1, d), scale3, shift3)


def _rms_kernel(x_ref, w_ref, o_ref):
    x = x_ref[...]
    o_ref[...] = x * lax.rsqrt(jnp.mean(x * x, axis=-1, keepdims=True) + EPS) * w_ref[...]


def _rmsnorm(x2, w, tt):
    r, d = x2.shape
    return pl.pallas_call(
        _rms_kernel,
        grid=(r // tt,),
        in_specs=[pl.BlockSpec((tt, d), lambda t: (t, 0)), pl.BlockSpec((1, d), lambda t: (0, 0))],
        out_specs=pl.BlockSpec((tt, d), lambda t: (t, 0)),
        out_shape=jax.ShapeDtypeStruct((r, d), F32),
        compiler_params=_cp("arbitrary"),
        name="rmsnorm",
    )(x2, w.reshape(1, d))


def _mm_kernel(x_ref, w_ref, o_ref):
    o_ref[...] = _dg(x_ref[...], w_ref[...])


def _matmul(x, w, layer, tm, tn, name):
    m, k = x.shape
    n = w.shape[2]
    return pl.pallas_call(
        _mm_kernel,
        grid=(m // tm, n // tn),
        in_specs=[pl.BlockSpec((tm, k), lambda i, j: (i, 0)),
                  pl.BlockSpec((None, k, tn), lambda i, j: (layer, 0, j))],
        out_specs=pl.BlockSpec((tm, tn), lambda i, j: (i, j)),
        out_shape=jax.ShapeDtypeStruct((m, n), F32),
        compiler_params=_cp("arbitrary", "arbitrary"),
        name=name,
    )(x, w)


def _mm_out_kernel(a_ref, b_ref, c_ref, d_ref, w_ref, res_ref, gate_ref, o_ref):
    g = a_ref.shape[1]
    acc = _dg(a_ref[...], w_ref[0:g, :])
    acc = acc + _dg(b_ref[...], w_ref[g:2 * g, :])
    acc = acc + _dg(c_ref[...], w_ref[2 * g:3 * g, :])
    acc = acc + _dg(d_ref[...], w_ref[3 * g:4 * g, :])
    o_ref[...] = res_ref[...] + gate_ref[0] * acc


def _gate_spec(gate3, tm, tn, rows_per_group):
    if gate3.shape[1] == 1:
        return pl.BlockSpec((1, 1, tn), lambda i, j, *_: ((i * tm) // rows_per_group, 0, j))
    return pl.BlockSpec((1, tm, tn), lambda i, j, *_: (0, i, j))


def _mm_out(parts, w, layer, res, gate3, tm, tn, rows_per_group):
    m, g = parts[0].shape
    n = w.shape[2]
    part_spec = pl.BlockSpec((tm, g), lambda i, j: (i, 0))
    return pl.pallas_call(
        _mm_out_kernel,
        grid=(m // tm, n // tn),
        in_specs=[part_spec, part_spec, part_spec, part_spec,
                  pl.BlockSpec((None, 4 * g, tn), lambda i, j: (layer, 0, j)),
                  pl.BlockSpec((tm, tn), lambda i, j: (i, j)),
                  _gate_spec(gate3, tm, tn, rows_per_group)],
        out_specs=pl.BlockSpec((tm, tn), lambda i, j: (i, j)),
        out_shape=jax.ShapeDtypeStruct((m, n), F32),
        compiler_params=_cp("arbitrary", "arbitrary"),
        name="mm_out",
    )(*parts, w, res, gate3)


def _down_kernel(a_ref, w_ref, res_ref, gate_ref, o_ref):
    k = pl.program_id(2)
    part = gate_ref[0] * _dg(a_ref[...], w_ref[...])

    @pl.when(k == 0)
    def _():
        o_ref[...] = res_ref[...] + part

    @pl.when(k > 0)
    def _():
        o_ref[...] += part


def _mm_down(act, w, layer, res, gate3, tm, tn, tk, rows_per_group):
    m, kk = act.shape
    n = w.shape[2]
    return pl.pallas_call(
        _down_kernel,
        grid=(m // tm, n // tn, kk // tk),
        in_specs=[pl.BlockSpec((tm, tk), lambda i, j, k: (i, k)),
                  pl.BlockSpec((None, tk, tn), lambda i, j, k: (layer, k, j)),
                  pl.BlockSpec((tm, tn), lambda i, j, k: (i, j)),
                  _gate_spec(gate3, tm, tn, rows_per_group)],
        out_specs=pl.BlockSpec((tm, tn), lambda i, j, k: (i, j)),
        out_shape=jax.ShapeDtypeStruct((m, n), F32),
        compiler_params=_cp("arbitrary", "arbitrary", "arbitrary"),
        name="mm_down",
    )(act, w, res, gate3)


UP_ROW_SPLIT = 2


def _up_prompt_kernel(x_ref, wg_ref, wv_ref, cw_ref, act_ref, tail_ref, carry_ref, *, tiles_per_seq):
    i = pl.program_id(1)
    tm = x_ref.shape[0]

    @pl.when(i % tiles_per_seq == 0)
    def _():
        carry_ref[...] = jnp.zeros_like(carry_ref)

    prev = carry_ref[...]
    cw = cw_ref[...]
    row8 = _iota(prev.shape, 0)
    th = tm // UP_ROW_SPLIT
    for r in range(UP_ROW_SPLIT):
        x = x_ref[r * th:(r + 1) * th, :]
        g = _dg(x, wg_ref[...])
        v = _dg(x, wv_ref[...])
        hg = g * cw[2:3, :] + _shift_rows(g, 1) * cw[1:2, :] + _shift_rows(g, 2) * cw[0:1, :]
        top = (jnp.where(row8 < 1, pltpu.roll(prev, 1, axis=0), 0.0) * cw[1:2, :]
               + jnp.where(row8 < 2, pltpu.roll(prev, 2, axis=0), 0.0) * cw[0:1, :])
        hg = jnp.concatenate([hg[0:8, :] + top, hg[8:, :]], axis=0)
        act_ref[r * th:(r + 1) * th, :] = (_silu(hg) * v).astype(act_ref.dtype)
        prev = g[th - 8:th, :]
    carry_ref[...] = prev
    tail_ref[0] = prev


def _up_prompt(h2, w_up, layer, conv_w, tm, tn, seq):
    m, k = h2.shape
    nf = w_up.shape[2] // 2
    nj = nf // tn
    kern = functools.partial(_up_prompt_kernel, tiles_per_seq=seq // tm)
    return pl.pallas_call(
        kern,
        grid=(nj, m // tm),
        in_specs=[pl.BlockSpec((tm, k), lambda j, i: (i, 0)),
                  pl.BlockSpec((None, k, tn), lambda j, i: (layer, 0, j)),
                  pl.BlockSpec((None, k, tn), lambda j, i: (layer, 0, nj + j)),
                  pl.BlockSpec((3, tn), lambda j, i: (0, j))],
        out_specs=[pl.BlockSpec((tm, tn), lambda j, i: (i, j)),
                   pl.BlockSpec((1, 8, tn), lambda j, i: ((i * tm) // seq, 0, j))],
        out_shape=[jax.ShapeDtypeStruct((m, nf), BF16),
                   jax.ShapeDtypeStruct((m // seq, 8, nf), F32)],
        scratch_shapes=[pltpu.VMEM((8, tn), F32)],
        compiler_params=_cp("arbitrary", "arbitrary"),
        name="up_prompt",
    )(h2, w_up, w_up, conv_w)


def _up_sample_kernel(x_ref, wg_ref, wv_ref, cw_ref, b0_ref, b1_ref, act_ref, g_ref):
    x = x_ref[...]
    g = _dg(x, wg_ref[...])
    v = _dg(x, wv_ref[...])
    cw = cw_ref[...]
    hg = b0_ref[...] * cw[0:1, :] + b1_ref[...] * cw[1:2, :] + g * cw[2:3, :]
    act_ref[...] = (_silu(hg) * v).astype(act_ref.dtype)
    g_ref[...] = g


def _up_sample(h2, w_up, layer, conv_w, buf0, buf1, tn):
    m, k = h2.shape
    nf = w_up.shape[2] // 2
    nj = nf // tn
    row_spec = pl.BlockSpec((m, tn), lambda j: (0, j))
    return pl.pallas_call(
        _up_sample_kernel,
        grid=(nj,),
        in_specs=[pl.BlockSpec((m, k), lambda j: (0, 0)),
                  pl.BlockSpec((None, k, tn), lambda j: (layer, 0, j)),
                  pl.BlockSpec((None, k, tn), lambda j: (layer, 0, nj + j)),
                  pl.BlockSpec((3, tn), lambda j: (0, j)),
                  row_spec, row_spec],
        out_specs=[row_spec, row_spec],
        out_shape=[jax.ShapeDtypeStruct((m, nf), BF16), jax.ShapeDtypeStruct((m, nf), F32)],
        compiler_params=_cp("arbitrary"),
        name="up_sample",
    )(h2, w_up, w_up, conv_w, buf0, buf1)


def _head_scalars(ps, alog_row, dt_row, fb_row):
    lane = _iota(ps.shape, 1)
    beta = jax.nn.sigmoid(ps)
    g = -jnp.exp(alog_row) * jax.nn.softplus(ps + dt_row)
    logf = jax.nn.log_sigmoid(ps + fb_row)
    return jnp.where(lane < L_G, beta, jnp.where(lane < L_F, g, logf))


def _aux_kernel(ps_ref, alog_ref, dt_ref, fb_ref, a1_ref, a2_ref, ft_ref):
    t = ps_ref.shape[1]
    a1 = _head_scalars(ps_ref[0], alog_ref[...], dt_ref[...], fb_ref[...])
    a1_ref[0] = a1
    c = GDN_CHUNK
    tril = (_iota((c, c), 0) >= _iota((c, c), 1)).astype(BF16)
    keep = (_iota((1, HEAD_DIM), 1) >= L_F).astype(F32)
    carry = jnp.zeros((1, HEAD_DIM), F32)
    for ci in range(t // c):
        cs = _mm_exact_lhs(tril, a1[ci * c:(ci + 1) * c, :]) + carry * keep
        a2_ref[0, ci * c:(ci + 1) * c, :] = cs
        carry = cs[c - 1:c, :]
    a2t = a2_ref[0].T
    ft_ref[0] = a2t[L_F:L_F + N_HEADS, :]


def _aux(ps3, alog_row, dt_row, fb_row):
    b, t, n = ps3.shape
    row = pl.BlockSpec((1, n), lambda i: (0, 0))
    blk = pl.BlockSpec((1, t, n), lambda i: (i, 0, 0))
    return pl.pallas_call(
        _aux_kernel,
        grid=(b,),
        in_specs=[blk, row, row, row],
        out_specs=[blk, blk, pl.BlockSpec((1, N_HEADS, t), lambda i: (i, 0, 0))],
        out_shape=[jax.ShapeDtypeStruct((b, t, n), F32), jax.ShapeDtypeStruct((b, t, n), F32),
                   jax.ShapeDtypeStruct((b, N_HEADS, t), F32)],
        compiler_params=_cp("arbitrary"),
        name="aux",
    )(ps3, alog_row, dt_row, fb_row)


GDN_HG = 2


GDN_INV_BLOCK = 16
GDN_PREP_CHUNKS = 4


def _unit_lower_inverses(lows):
    n = lows[0].shape[0]
    bsz = GDN_INV_BLOCK
    ri = _iota((n, n), 0)
    ci = _iota((n, n), 1)
    eye = (ri == ci).astype(F32)
    nb = n // 8
    in_diag = ri // bsz == ci // bsz
    xbs = [[eye[8 * r:8 * r + 8, :] for r in range(nb)] for _ in lows]
    lbs = [[jnp.where(in_diag, low, 0.0)[8 * r:8 * r + 8, :] for r in range(nb)] for low in lows]
    for j in range(bsz - 1):
        for xb, lb in zip(xbs, lbs):
            for b in range(n // bsz):
                col = b * bsz + j
                rj = xb[col // 8][col % 8:col % 8 + 1, :]
                for r in range(col // 8, (b + 1) * bsz // 8):
                    xb[r] = xb[r] - lb[r][:, col:col + 1] * rj
    invs = [jnp.concatenate(xb, axis=0) for xb in xbs]
    s = bsz
    while s < n:
        pair = ri // (2 * s) == ci // (2 * s)
        same = ri // s == ci // s
        belows = [jnp.where(pair, jnp.where(same, 0.0, low), 0.0) for low in lows]
        left = [_mm3(inv, below) for inv, below in zip(invs, belows)]
        invs = [inv - _mm3(lt, inv) for inv, lt in zip(invs, left)]
        s *= 2
    return invs


def _gdn_prompt_kernel(q_ref, k_ref, v_ref, z_ref, wq_ref, wk_ref, wv_ref, a1_ref, a2_ref, nw_ref,
                       o_ref, s_ref, qs, ks, vs, qp, op, ap, bp, egl):
    t = q_ref.shape[0]
    c = GDN_CHUNK
    nc = t // c
    hbase = pl.program_id(1) * GDN_HG

    for h in range(GDN_HG):
        sl = slice(h * HEAD_DIM, (h + 1) * HEAD_DIM)
        q = _silu(_causal_conv(q_ref[:, sl], wq_ref[:, sl]))
        k = _silu(_causal_conv(k_ref[:, sl], wk_ref[:, sl]))
        v = _silu(_causal_conv(v_ref[:, sl], wv_ref[:, sl]))
        q = q * lax.rsqrt(jnp.sum(q * q, axis=-1, keepdims=True) + EPS) * (HEAD_DIM ** -0.5)
        k = k * lax.rsqrt(jnp.sum(k * k, axis=-1, keepdims=True) + EPS)
        qs[h] = q
        ks[h] = k
        vs[h] = v

    ri = _iota((c, c), 0)
    ci_ = _iota((c, c), 1)
    causal = ri >= ci_
    strict = ri > ci_
    eye = ri == ci_

    def prep(it, carry):
        chains = [(it * GDN_PREP_CHUNKS + cc, h) for cc in range(GDN_PREP_CHUNKS) for h in range(GDN_HG)]
        loaded = []
        for ci, h in chains:
            rows = pl.ds(pl.multiple_of(ci * c, c), c)
            loaded.append((a1_ref[0, rows, :], a2_ref[0, rows, :], qs[h, rows, :], ks[h, rows, :], vs[h, rows, :]))
        nch = range(len(chains))
        beta = [_lane_pick(loaded[i][0], L_BETA + hbase + chains[i][1]) for i in nch]
        gc = [_lane_pick(loaded[i][1], L_G + hbase + chains[i][1]) for i in nch]
        qv = [loaded[i][2] for i in nch]
        kv = [loaded[i][3] for i in nch]
        vv = [loaded[i][4] for i in nch]
        decay = []
        for i in nch:
            gc_row = jnp.sum(jnp.where(eye, gc[i], 0.0), axis=0, keepdims=True)
            decay.append(jnp.where(causal, jnp.exp(jnp.where(causal, gc[i] - gc_row, 0.0)), 0.0))
        kb = [kv[i] * beta[i] for i in nch]
        lows = [_mm1(kb[i], kv[i], NT) * jnp.where(strict, decay[i], 0.0) for i in nch]
        qk = [_mm1(qv[i], kv[i], NT) * decay[i] for i in nch]
        invs = _unit_lower_inverses(lows)
        eg = [jnp.exp(gc[i]) for i in nch]
        wu = [_mm3(invs[i], jnp.concatenate([kb[i] * eg[i], vv[i] * beta[i]], axis=1)) for i in nch]
        g_last = [gc[i][c - 1:c, :] for i in nch]
        kd = [kv[i] * jnp.exp(g_last[i] - gc[i]) for i in nch]
        kwu = [_mm1(kd[i], wu[i], TN) for i in nch]
        qwu = [_mm1(qk[i], wu[i]) for i in nch]
        results = [((qv[i] * eg[i] - qwu[i][:, :HEAD_DIM]).astype(BF16), qwu[i][:, HEAD_DIM:],
                    (-kwu[i][:, :HEAD_DIM]).astype(BF16), kwu[i][:, HEAD_DIM:],
                    jnp.broadcast_to(jnp.exp(g_last[i]), (8, HEAD_DIM))) for i in nch]
        for (ci, h), (qp_c, op_c, ap_c, bp_c, e_c) in zip(chains, results):
            rows = pl.ds(pl.multiple_of(ci * c, c), c)
            srows = pl.ds(pl.multiple_of(ci * HEAD_DIM, HEAD_DIM), HEAD_DIM)
            qp[h, rows, :] = qp_c
            op[h, rows, :] = op_c
            ap[h, srows, :] = ap_c
            bp[h, srows, :] = bp_c
            egl[h, pl.ds(pl.multiple_of(ci * 8, 8), 8), :] = e_c
        return carry

    lax.fori_loop(0, nc // GDN_PREP_CHUNKS, prep, 0)

    nw = nw_ref[...]

    def sweep(ci, states):
        rows = pl.ds(pl.multiple_of(ci * c, c), c)
        srows = pl.ds(pl.multiple_of(ci * HEAD_DIM, HEAD_DIM), HEAD_DIM)
        new_states = []
        for h in range(GDN_HG):
            s = states[h]
            sb = s.astype(BF16)
            o = _dg(qp[h, rows, :], sb) + op[h, rows, :]
            e_last = egl[h, pl.ds(pl.multiple_of(ci * 8, 8), 8), :][0:1, :]
            new_states.append(s * e_last + (_dg(ap[h, srows, :], sb) + bp[h, srows, :]))
            o = o * lax.rsqrt(jnp.mean(o * o, axis=-1, keepdims=True) + EPS) * nw
            z = z_ref[rows, h * HEAD_DIM:(h + 1) * HEAD_DIM]
            o_ref[rows, h * HEAD_DIM:(h + 1) * HEAD_DIM] = (o * _silu(z)).astype(o_ref.dtype)
        return tuple(new_states)

    init = tuple(jnp.zeros((HEAD_DIM, HEAD_DIM), F32) for _ in range(GDN_HG))
    final = lax.fori_loop(0, nc, sweep, init)
    for h in range(GDN_HG):
        s_ref[0, h] = final[h]


def _gdn_prompt(proj, a1, a2, conv_w, norm_w, b, t):
    hw = GDN_HG * HEAD_DIM
    ng = W_GROUP // hw
    col = lambda off: pl.BlockSpec((t, hw), lambda i, j: (i, off * ng + j))
    wcol = lambda off: pl.BlockSpec((4, hw), lambda i, j: (0, off * ng + j))
    aux = pl.BlockSpec((1, t, HEAD_DIM), lambda i, j: (i, 0, 0))
    nc = t // GDN_CHUNK
    scr = lambda rows, dt: pltpu.VMEM((GDN_HG, rows, HEAD_DIM), dt)
    return pl.pallas_call(
        _gdn_prompt_kernel,
        grid=(b, ng),
        in_specs=[col(0), col(1), col(2), col(3), wcol(0), wcol(1), wcol(2), aux, aux,
                  pl.BlockSpec((1, HEAD_DIM), lambda i, j: (0, 0))],
        out_specs=[pl.BlockSpec((t, hw), lambda i, j: (i, j)),
                   pl.BlockSpec((1, GDN_HG, HEAD_DIM, HEAD_DIM), lambda i, j: (i, j, 0, 0))],
        out_shape=[jax.ShapeDtypeStruct((b * t, W_GROUP), BF16),
                   jax.ShapeDtypeStruct((b, N_HEADS, HEAD_DIM, HEAD_DIM), F32)],
        scratch_shapes=[scr(t, F32), scr(t, F32), scr(t, F32), scr(t, BF16), scr(t, F32),
                        scr(nc * HEAD_DIM, BF16), scr(nc * HEAD_DIM, F32), scr(nc * 8, F32)],
        compiler_params=_cp("arbitrary", "arbitrary"),
        name="gdn_prompt",
    )(proj, proj, proj, proj, conv_w, conv_w, conv_w, a1, a2, norm_w.reshape(1, HEAD_DIM))


def _sconv_prompt_kernel(x_ref, gb_ref, gc_ref, w_ref, o_ref, tail_ref):
    u = gc_ref[...] * x_ref[...]
    t = u.shape[0]
    o_ref[...] = (gb_ref[...] * _causal_conv(u, w_ref[...])).astype(o_ref.dtype)
    tail_ref[0] = u[t - 8:t, :]


def _sconv_prompt(proj, w, b, t):
    tc = 256
    nb = W_GROUP // tc
    base = M_B // tc
    col = lambda k: pl.BlockSpec((t, tc), lambda i, j: (i, base + k * nb + j))
    return pl.pallas_call(
        _sconv_prompt_kernel,
        grid=(b, nb),
        in_specs=[col(0), col(1), col(2), pl.BlockSpec((3, tc), lambda i, j: (0, j))],
        out_specs=[pl.BlockSpec((t, tc), lambda i, j: (i, j)),
                   pl.BlockSpec((1, 8, tc), lambda i, j: (i, 0, j))],
        out_shape=[jax.ShapeDtypeStruct((b * t, W_GROUP), BF16),
                   jax.ShapeDtypeStruct((b, 8, W_GROUP), F32)],
        compiler_params=_cp("arbitrary", "arbitrary"),
        name="sconv_prompt",
    )(proj, proj, proj, w)


FOX_TQ = 512


def _fox_prompt_kernel(q_ref, k_ref, v_ref, a2_ref, ft_ref, o_ref):
    h = pl.program_id(1)
    qi = pl.program_id(2)
    q = q_ref[...].astype(BF16)
    tq = q.shape[0]
    tk = tq
    fq = _lane_pick(a2_ref[0], L_F + h)
    qpos = qi * tq + _iota((tq, tk), 0)
    kofs = _iota((tq, tk), 1)
    hsel = _iota((N_HEADS, tk), 0) == h

    def body(j, carry):
        m, l, acc = carry
        start = pl.multiple_of(j * tk, tk)
        k = k_ref[pl.ds(start, tk), :].astype(BF16)
        v = v_ref[pl.ds(start, tk), :].astype(BF16)
        fk = jnp.sum(jnp.where(hsel, ft_ref[0, :, pl.ds(start, tk)], 0.0), axis=0, keepdims=True)
        s = _dg(q, k, NT) * (HEAD_DIM ** -0.5) + fq - fk
        s = jnp.where(j * tk + kofs <= qpos, s, -jnp.inf)
        m_new = jnp.maximum(m, jnp.max(s, axis=-1, keepdims=True))
        alpha = jnp.exp(m - m_new)
        p = jnp.exp(s - m_new)
        l = alpha * l + jnp.sum(p, axis=-1, keepdims=True)
        acc = alpha * acc + _dg(p.astype(BF16), v)
        return m_new, l, acc

    init = (jnp.full((tq, 1), -jnp.inf, F32), jnp.zeros((tq, 1), F32), jnp.zeros((tq, HEAD_DIM), F32))
    _, l, acc = lax.fori_loop(0, qi + 1, body, init)
    o_ref[...] = (acc / l).astype(o_ref.dtype)


def _fox_prompt(proj, a2, ft, b, t):
    tq = min(FOX_TQ, t)
    nq = t // tq
    base = M_C // HEAD_DIM
    return pl.pallas_call(
        _fox_prompt_kernel,
        grid=(b, N_HEADS, nq),
        in_specs=[pl.BlockSpec((tq, HEAD_DIM), lambda i, h, q: (i * nq + q, base + h)),
                  pl.BlockSpec((t, HEAD_DIM), lambda i, h, q: (i, base + N_HEADS + h)),
                  pl.BlockSpec((t, HEAD_DIM), lambda i, h, q: (i, base + 2 * N_HEADS + h)),
                  pl.BlockSpec((1, tq, HEAD_DIM), lambda i, h, q: (i, q, 0)),
                  pl.BlockSpec((1, N_HEADS, t), lambda i, h, q: (i, 0, 0))],
        out_specs=pl.BlockSpec((tq, HEAD_DIM), lambda i, h, q: (i * nq + q, h)),
        out_shape=jax.ShapeDtypeStruct((b * t, W_GROUP), BF16),
        compiler_params=_cp("arbitrary", "arbitrary", "arbitrary"),
        name="fox_prompt",
    )(proj, proj, proj, a2, ft)


def _lru_gates(xd, wr, br, wi, bi, lam):
    r = jax.nn.sigmoid(_mm3(xd, wr) + br)
    i = jax.nn.sigmoid(_mm3(xd, wi) + bi)
    log_a = -LRU_C * r * jax.nn.softplus(-lam)
    a = jnp.exp(log_a)
    th = jnp.tanh(log_a)
    one_minus_a2 = -2.0 * th / (1.0 - th)
    return a, jnp.sqrt(one_minus_a2) * (i * xd)


def _lru_prompt_kernel(x_ref, y_ref, cw_ref, cb_ref, wr_ref, br_ref, wi_ref, bi_ref, lam_ref,
                       o_ref, tail_ref):
    t = x_ref.shape[0]
    xd = _causal_conv(x_ref[...], cw_ref[...]) + cb_ref[...]
    a, bb = _lru_gates(xd, wr_ref[0], br_ref[...], wi_ref[0], bi_ref[...], lam_ref[...])
    row = _iota(a.shape, 0)
    s = 1
    while s < t:
        a_s = jnp.where(row < s, 1.0, pltpu.roll(a, s, axis=0))
        b_s = jnp.where(row < s, 0.0, pltpu.roll(bb, s, axis=0))
        bb = a * b_s + bb
        a = a * a_s
        s *= 2
    o_ref[...] = (jax.nn.gelu(y_ref[...]) * bb).astype(o_ref.dtype)
    tail_ref[0] = bb[t - 8:t, :]


def _lru_prompt(proj, cw, cb, wr, br, wi, bi, lam, b, t):
    n = HEAD_DIM
    nb = W_GROUP // n
    base = M_D // n
    vec = lambda: pl.BlockSpec((1, n), lambda i, j: (0, j))
    mat = lambda: pl.BlockSpec((1, n, n), lambda i, j: (j, 0, 0))
    r1 = lambda a: a.reshape(1, W_GROUP)
    return pl.pallas_call(
        _lru_prompt_kernel,
        grid=(b, nb),
        in_specs=[pl.BlockSpec((t, n), lambda i, j: (i, base + j)),
                  pl.BlockSpec((t, n), lambda i, j: (i, base + nb + j)),
                  pl.BlockSpec((4, n), lambda i, j: (0, j)),
                  vec(), mat(), vec(), mat(), vec(), vec()],
        out_specs=[pl.BlockSpec((t, n), lambda i, j: (i, j)),
                   pl.BlockSpec((1, 8, n), lambda i, j: (i, 0, j))],
        out_shape=[jax.ShapeDtypeStruct((b * t, W_GROUP), BF16),
                   jax.ShapeDtypeStruct((b, 8, W_GROUP), F32)],
        compiler_params=_cp("arbitrary", "arbitrary"),
        name="lru_prompt",
    )(proj, proj, cw, r1(cb), wr, r1(br), wi, r1(bi), r1(lam))


def _gdn_sample_kernel(q_ref, k_ref, v_ref, z_ref, b0_ref, b1_ref, b2_ref, cw_ref, ps_ref,
                       alog_ref, dt_ref, fb_ref, nw_ref, s_ref, o_ref, so_ref, hs_ref):
    i = pl.program_id(0)
    cw = cw_ref[...]
    hs = _head_scalars(ps_ref[...], alog_ref[...], dt_ref[...], fb_ref[...])
    hs_ref[...] = hs
    hs_i = jnp.sum(jnp.where(_iota(hs.shape, 0) == i, hs, 0.0), axis=0, keepdims=True)
    row_i = lambda ref: ref[pl.ds(i, 1), :]
    b0 = row_i(b0_ref)
    b1 = row_i(b1_ref)
    b2 = row_i(b2_ref)
    eye = _iota((HEAD_DIM, HEAD_DIM), 0) == _iota((HEAD_DIM, HEAD_DIM), 1)
    nw = nw_ref[...]

    def conv(x_row, off):
        w = cw[:, off:off + W_GROUP]
        bo = slice(off, off + W_GROUP)
        return _silu(b0[:, bo] * w[0:1, :] + b1[:, bo] * w[1:2, :] + b2[:, bo] * w[2:3, :] + x_row * w[3:4, :])

    q_all = conv(row_i(q_ref), 0)
    k_all = conv(row_i(k_ref), W_GROUP)
    v_all = conv(row_i(v_ref), 2 * W_GROUP)
    z_all = row_i(z_ref)
    for h in range(N_HEADS):
        sl = slice(h * HEAD_DIM, (h + 1) * HEAD_DIM)
        q = q_all[:, sl]
        k = k_all[:, sl]
        v = v_all[:, sl]
        q = q * lax.rsqrt(jnp.sum(q * q, axis=-1, keepdims=True) + EPS) * (HEAD_DIM ** -0.5)
        k = k * lax.rsqrt(jnp.sum(k * k, axis=-1, keepdims=True) + EPS)
        beta = hs_i[:, L_BETA + h:L_BETA + h + 1]
        g = hs_i[:, L_G + h:L_G + h + 1]
        k_col = jnp.sum(jnp.where(eye, k, 0.0), axis=1, keepdims=True)
        q_col = jnp.sum(jnp.where(eye, q, 0.0), axis=1, keepdims=True)
        s = s_ref[0, h] * jnp.exp(g)
        delta = (v - jnp.sum(k_col * s, axis=0, keepdims=True)) * beta
        s = s + k_col * delta
        so_ref[0, h] = s
        o = jnp.sum(q_col * s, axis=0, keepdims=True)
        o = o * lax.rsqrt(jnp.mean(o * o, axis=-1, keepdims=True) + EPS) * nw
        o_ref[0, :, sl] = o * _silu(z_all[:, sl])


def _gdn_sample(proj, ps, buf, conv_w, alog_row, dt_row, fb_row, norm_w, state, bs):
    rows = proj.shape[0]
    col = lambda k: pl.BlockSpec((rows, W_GROUP), lambda i: (0, k))
    full = lambda a: pl.BlockSpec(a.shape, lambda i: (0,) * a.ndim)
    prow = pl.BlockSpec((1, HEAD_DIM), lambda i: (0, 0))
    sblk = pl.BlockSpec((1, N_HEADS, HEAD_DIM, HEAD_DIM), lambda i: (i, 0, 0, 0))
    return pl.pallas_call(
        _gdn_sample_kernel,
        grid=(bs,),
        in_specs=[col(0), col(1), col(2), col(3), full(buf[0]), full(buf[1]), full(buf[2]),
                  full(conv_w), pl.BlockSpec((rows, HEAD_DIM), lambda i: (0, 0)), prow, prow, prow, prow, sblk],
        out_specs=[pl.BlockSpec((1, 1, W_GROUP), lambda i: (i, 0, 0)), sblk,
                   pl.BlockSpec((rows, HEAD_DIM), lambda i: (0, 0))],
        out_shape=[jax.ShapeDtypeStruct((bs, 1, W_GROUP), F32),
                   jax.ShapeDtypeStruct(state.shape, F32),
                   jax.ShapeDtypeStruct((rows, HEAD_DIM), F32)],
        compiler_params=_cp("arbitrary"),
        name="gdn_sample",
    )(proj, proj, proj, proj, buf[0], buf[1], buf[2], conv_w, ps, alog_row, dt_row, fb_row,
      norm_w.reshape(1, HEAD_DIM), state)


def _bd_sample_kernel(xb_ref, gb_ref, gc_ref, sb0_ref, sb1_ref, sw_ref,
                      xd_ref, yd_ref, lb0_ref, lb1_ref, lb2_ref, lh_ref, cw_ref, cb_ref,
                      wr_ref, br_ref, wi_ref, bi_ref, lam_ref,
                      ob_ref, u_ref, od_ref, h_ref):
    sw = sw_ref[...]
    u = gc_ref[...] * xb_ref[...]
    ob_ref[...] = gb_ref[...] * (sb0_ref[...] * sw[0:1, :] + sb1_ref[...] * sw[1:2, :] + u * sw[2:3, :])
    u_ref[...] = u
    cw = cw_ref[...]
    xd = (lb0_ref[...] * cw[0:1, :] + lb1_ref[...] * cw[1:2, :] + lb2_ref[...] * cw[2:3, :]
          + xd_ref[...] * cw[3:4, :] + cb_ref[...])
    n = HEAD_DIM
    for j in range(W_GROUP // n):
        sl = slice(j * n, (j + 1) * n)
        a, bb = _lru_gates(xd[:, sl], wr_ref[j], br_ref[:, sl], wi_ref[j], bi_ref[:, sl], lam_ref[:, sl])
        hn = a * lh_ref[:, sl] + bb
        h_ref[:, sl] = hn
        od_ref[:, sl] = jax.nn.gelu(yd_ref[:, sl]) * hn


def _bd_sample(proj, sbuf, sw, lbuf, lh, cw, cb, wr, br, wi, bi, lam):
    rows = proj.shape[0]
    r1 = lambda a: a.reshape(1, W_GROUP)
    pcol = lambda c: pl.BlockSpec((rows, W_GROUP), lambda i: (0, c // W_GROUP))
    full = lambda a: pl.BlockSpec(a.shape, lambda i: (0,) * a.ndim)
    ins = [proj, proj, proj, sbuf[0], sbuf[1], sw, proj, proj, lbuf[0], lbuf[1], lbuf[2], lh, cw,
           r1(cb), wr, r1(br), wi, r1(bi), r1(lam)]
    specs = [pcol(M_B), pcol(M_B + W_GROUP), pcol(M_B + 2 * W_GROUP)] + [full(a) for a in ins[3:6]] \
        + [pcol(M_D), pcol(M_D + W_GROUP)] + [full(a) for a in ins[8:]]
    out = jax.ShapeDtypeStruct((rows, W_GROUP), F32)
    ospec = pl.BlockSpec((rows, W_GROUP), lambda i: (0, 0))
    return pl.pallas_call(
        _bd_sample_kernel,
        grid=(1,),
        in_specs=specs,
        out_specs=[ospec] * 4,
        out_shape=[out] * 4,
        compiler_params=_cp("arbitrary"),
        name="bd_sample",
    )(*ins)


FOX_PAGES_PER_STEP = 8


def _fox_sample_kernel(pt_ref, q_ref, kn_ref, vn_ref, r0_ref, *refs):
    del pt_ref
    n = FOX_PAGES_PER_STEP
    kp_refs, vp_refs, lf_refs = refs[0:n], refs[n:2 * n], refs[2 * n:3 * n]
    o_ref, m_sc, l_sc, acc_sc, r_sc = refs[3 * n:]
    i = pl.program_id(0)
    p = pl.program_id(1)
    npg = pl.num_programs(1)
    scale = HEAD_DIM ** -0.5
    q = q_ref[i]

    @pl.when(p == 0)
    def _():
        m_sc[...] = jnp.sum(q * kn_ref[i], axis=-1, keepdims=True) * scale
        l_sc[...] = jnp.ones_like(l_sc)
        acc_sc[...] = vn_ref[i]
        r_sc[...] = r0_ref[pl.ds(i, 1), :]

    m = m_sc[...]
    l = l_sc[...]
    acc = acc_sc[...]
    r = r_sc[...]
    pg, nh, hd = kp_refs[0].shape
    flat = pg * nh
    qb = q.astype(BF16)
    lane = _iota((1, flat), 1)
    own = _iota((nh, flat), 1) % nh == _iota((nh, flat), 0)
    pages = range(n)
    lfs = [lf_refs[t][...] for t in pages]
    suffix = list(lfs)
    total = list(lfs)
    step = nh
    while step < flat:
        suffix = [x + jnp.where(lane + step < flat, pltpu.roll(x, flat - step, axis=1), 0.0) for x in suffix]
        total = [x + pltpu.roll(x, step, axis=1) for x in total]
        step *= 2
    scores = [_dg(qb, kp_refs[t][...].reshape(flat, hd).astype(BF16), NT) * scale for t in pages]
    s = []
    for t in pages:
        s.append(jnp.where(own, scores[t] + ((suffix[t] - lfs[t]) + r), -jnp.inf))
        r = r + total[t]
    m_new = m
    for t in pages:
        m_new = jnp.maximum(m_new, jnp.max(s[t], axis=-1, keepdims=True))
    alpha = jnp.exp(m - m_new)
    pw = [jnp.exp(s[t] - m_new) for t in pages]
    l = alpha * l
    acc = alpha * acc
    for t in pages:
        l = l + jnp.sum(pw[t], axis=-1, keepdims=True)
        acc = acc + _dg(pw[t].astype(BF16), vp_refs[t][...].reshape(flat, hd).astype(BF16))
    m_sc[...] = m_new
    l_sc[...] = l
    acc_sc[...] = acc
    r_sc[...] = r

    @pl.when(p == npg - 1)
    def _():
        o_ref[0] = acc / l


def _fox_sample(q3, kn3, vn3, logf_new, page_table, ck, cv, clf, layer, bs):
    rows = q3.shape[0]
    n_pages = page_table.shape[1]
    page = ck.shape[2]
    flat = page * N_HEADS
    n = FOX_PAGES_PER_STEP
    assert n_pages % n == 0
    r0 = jnp.tile(logf_new, (1, page))
    clf_flat = clf.reshape(clf.shape[0], clf.shape[1], 1, flat)
    new_spec = pl.BlockSpec((rows, N_HEADS, HEAD_DIM), lambda i, p, pt: (0, 0, 0))

    def kv_spec(t):
        return pl.BlockSpec((None, None, page, N_HEADS, HEAD_DIM),
                            lambda i, p, pt: (layer, pt[i, n_pages - 1 - (p * n + t)], 0, 0, 0))

    def lf_spec(t):
        return pl.BlockSpec((None, None, 1, flat),
                            lambda i, p, pt: (layer, pt[i, n_pages - 1 - (p * n + t)], 0, 0))

    slots = list(range(n))
    grid_spec = pltpu.PrefetchScalarGridSpec(
        num_scalar_prefetch=1,
        grid=(bs, n_pages // n),
        in_specs=[new_spec, new_spec, new_spec, pl.BlockSpec((rows, flat), lambda i, p, pt: (0, 0))]
        + [kv_spec(t) for t in slots] + [kv_spec(t) for t in slots] + [lf_spec(t) for t in slots],
        out_specs=pl.BlockSpec((1, N_HEADS, HEAD_DIM), lambda i, p, pt: (i, 0, 0)),
        scratch_shapes=[pltpu.VMEM((N_HEADS, 1), F32), pltpu.VMEM((N_HEADS, 1), F32),
                        pltpu.VMEM((N_HEADS, HEAD_DIM), F32), pltpu.VMEM((1, flat), F32)],
    )
    return pl.pallas_call(
        _fox_sample_kernel,
        grid_spec=grid_spec,
        out_shape=jax.ShapeDtypeStruct((bs, N_HEADS, HEAD_DIM), F32),
        compiler_params=_cp("arbitrary", "arbitrary"),
        name="fox_sample",
    )(page_table, q3, kn3, vn3, r0, *([ck] * n), *([cv] * n), *([clf_flat] * n))


def _prep_weights(w_in, w_out, w_up, w_down, ffn_conv_w):
    w_main = jnp.concatenate([w_in[:, :, 0:OFF_A_BETA], w_in[:, :, OFF_B:OFF_C_F], w_in[:, :, OFF_D:N_IN]],
                             axis=2).astype(BF16)
    w_small = jnp.concatenate([w_in[:, :, OFF_A_BETA:OFF_B], w_in[:, :, OFF_C_F:OFF_D],
                               jnp.zeros(w_in.shape[:2] + (HEAD_DIM - 3 * N_HEADS,), w_in.dtype)],
                              axis=2).astype(BF16)
    padc = ((0, 0), (0, 0), (0, D_FF_PAD - D_FF))
    w_up_p = jnp.concatenate([jnp.pad(w_up[:, :, :D_FF].astype(BF16), padc),
                              jnp.pad(w_up[:, :, D_FF:].astype(BF16), padc)], axis=2)
    w_down_p = jnp.pad(w_down.astype(BF16), ((0, 0), (0, D_FF_PAD - D_FF), (0, 0)))
    return dict(w_main=w_main, w_small=w_small, w_out=w_out.astype(BF16), w_up=w_up_p, w_down=w_down_p,
                ffn_cw=jnp.pad(ffn_conv_w, padc))


def _lane_row(v, off):
    return jnp.zeros((1, HEAD_DIM), F32).at[0, off:off + N_HEADS].set(v)


def _pad_rows(a, rows):
    return jnp.pad(a, ((0, rows - a.shape[0]),) + ((0, 0),) * (a.ndim - 1))


def kernel(x_prompt, x_sample, cache_k, cache_v, cache_logf, state_gdn, state_gdn_conv, state_sconv,
           state_lru_h, state_lru_conv, state_ffn_conv, page_table, c_prompt, c_sample, ada_w, ada_b,
           ln1_w, ln2_w, w_in, gdn_conv_w, gdn_a_log, gdn_dt_bias, gdn_norm_w, sconv_w, fox_f_bias,
           lru_conv_w, lru_conv_b, lru_wr, lru_br, lru_wi, lru_bi, lru_lambda, w_out, ffn_conv_w,
           w_up, w_down, final_norm_w):
    bp, t, d = x_prompt.shape
    bs = x_sample.shape[0]
    depth = w_in.shape[0]
    sr = SAMPLE_ROWS
    tm = min(1024, t)
    tnorm = min(512, t)
    c16 = _pad_rows(jnp.concatenate([c_prompt, c_sample], axis=0), sr)

    xp = x_prompt.reshape(bp * t, d)
    xs = _pad_rows(x_sample.reshape(bs, d), sr)
    p_states, s_states = [], []
    w = _prep_weights(w_in, w_out, w_up, w_down, ffn_conv_w)
    for l in range(depth):
        alog_row = _lane_row(gdn_a_log[l], L_G)
        dt_row = _lane_row(gdn_dt_bias[l], L_G)
        fb_row = _lane_row(fox_f_bias[l], L_F)
        ffn_cw = w["ffn_cw"][l]
        mod = _ada(c16, ada_w, ada_b, l).reshape(sr, 6, d)
        pm = [mod[0:bp, k][:, None, :] for k in range(6)]
        sm = [_pad_rows(mod[bp:bp + bs, k], sr)[None] for k in range(6)]

        h = _norm_mod(xp.reshape(bp, t, d), ln1_w[l], pm[1], pm[0], tnorm).reshape(bp * t, d)
        proj = _matmul(h, w["w_main"], l, tm, 1024, "mm_in")
        ps = _matmul(h, w["w_small"], l, tm, HEAD_DIM, "mm_in_small")
        a1, a2, ft = _aux(ps.reshape(bp, t, HEAD_DIM), alog_row, dt_row, fb_row)
        oa, p_gdn = _gdn_prompt(proj, a1, a2, gdn_conv_w[l], gdn_norm_w[l], bp, t)
        ob, sconv_tail = _sconv_prompt(proj, sconv_w[l], bp, t)
        oc = _fox_prompt(proj, a2, ft, bp, t)
        od, lru_tail = _lru_prompt(proj, lru_conv_w[l], lru_conv_b[l], lru_wr[l], lru_br[l], lru_wi[l],
                                   lru_bi[l], lru_lambda[l], bp, t)
        xp = _mm_out([oa, ob, oc, od], w["w_out"], l, xp, pm[2], tm, 1024, t)
        h2 = _norm_mod(xp.reshape(bp, t, d), ln2_w[l], pm[4], pm[3], tnorm).reshape(bp * t, d)
        act, ffn_tail = _up_prompt(h2, w["w_up"], l, ffn_cw, tm, 512, t)
        xp = _mm_down(act, w["w_down"], l, xp, pm[5], min(512, t), 512, D_FF_PAD, t)

        proj3 = proj.reshape(bp, t, N_MAIN)
        p_states.append((
            proj3[:, :, M_C + W_GROUP:M_C + 2 * W_GROUP].reshape(bp, t, N_HEADS, HEAD_DIM),
            proj3[:, :, M_C + 2 * W_GROUP:M_C + 3 * W_GROUP].reshape(bp, t, N_HEADS, HEAD_DIM),
            a1[:, :, L_F:L_F + N_HEADS],
            p_gdn,
            proj3[:, t - 3:, M_A:M_A + 3 * W_GROUP],
            sconv_tail[:, 6:8, :],
            lru_tail[:, 7, :],
            proj3[:, t - 3:, M_D:M_D + W_GROUP],
            ffn_tail[:, 6:8, :D_FF],
        ))

        hs_in = _norm_mod(xs[None], ln1_w[l], sm[1], sm[0], sr)[0]
        sproj = _matmul(hs_in, w["w_main"], l, sr, 2048, "mm_in_s")
        sps = _matmul(hs_in, w["w_small"], l, sr, HEAD_DIM, "mm_in_small_s")
        gbuf = [_pad_rows(state_gdn_conv[l][:, j, :], sr) for j in range(3)]
        soa, s_gdn, shs = _gdn_sample(sproj, sps, gbuf, gdn_conv_w[l], alog_row, dt_row,
                                      fb_row, gdn_norm_w[l], state_gdn[l], bs)
        sbuf = [_pad_rows(state_sconv[l][:, j, :], sr) for j in range(2)]
        lbuf = [_pad_rows(state_lru_conv[l][:, j, :], sr) for j in range(3)]
        sob, s_u, sod, s_h = _bd_sample(sproj, sbuf, sconv_w[l], lbuf, _pad_rows(state_lru_h[l], sr),
                                        lru_conv_w[l], lru_conv_b[l], lru_wr[l], lru_br[l], lru_wi[l],
                                        lru_bi[l], lru_lambda[l])
        heads3 = lambda off: sproj[:, off:off + W_GROUP].reshape(sr, N_HEADS, HEAD_DIM)
        soc = _fox_sample(heads3(M_C), heads3(M_C + W_GROUP), heads3(M_C + 2 * W_GROUP),
                          shs[:, L_F:L_F + N_HEADS], page_table, cache_k, cache_v, cache_logf, l, bs)
        parts = [_pad_rows(soa.reshape(bs, W_GROUP), sr).astype(BF16), sob.astype(BF16),
                 _pad_rows(soc.reshape(bs, W_GROUP), sr).astype(BF16), sod.astype(BF16)]
        xs = _mm_out(parts, w["w_out"], l, xs, sm[2], sr, 2048, sr)
        sh2 = _norm_mod(xs[None], ln2_w[l], sm[4], sm[3], sr)[0]
        fbuf = [_pad_rows(jnp.pad(state_ffn_conv[l][:, j, :], ((0, 0), (0, D_FF_PAD - D_FF))), sr)
                for j in range(2)]
        sact, s_g = _up_sample(sh2, w["w_up"], l, ffn_cw, fbuf[0], fbuf[1], 1024)
        xs = _mm_down(sact, w["w_down"], l, xs, sm[5], sr, 2048, D_FF_PAD // 4, sr)

        sp = sproj[:bs]
        s_states.append((
            sp[:, M_C + W_GROUP:M_C + 2 * W_GROUP].reshape(bs, 1, N_HEADS, HEAD_DIM),
            sp[:, M_C + 2 * W_GROUP:M_C + 3 * W_GROUP].reshape(bs, 1, N_HEADS, HEAD_DIM),
            shs[:bs, L_F:L_F + N_HEADS].reshape(bs, 1, N_HEADS),
            s_gdn,
            jnp.concatenate([state_gdn_conv[l][:, 1:, :], sp[:, None, M_A:M_A + 3 * W_GROUP]], axis=1),
            jnp.concatenate([state_sconv[l][:, 1:, :], s_u[:bs, None, :]], axis=1),
            s_h[:bs],
            jnp.concatenate([state_lru_conv[l][:, 1:, :], sp[:, None, M_D:M_D + W_GROUP]], axis=1),
            jnp.concatenate([state_ffn_conv[l][:, 1:, :], s_g[:bs, None, :D_FF]], axis=1),
        ))

    y_prompt = _rmsnorm(xp, final_norm_w, tnorm).reshape(bp, t, d)
    y_sample = _rmsnorm(xs, final_norm_w, sr)[:bs].reshape(bs, 1, d)
    p_out = tuple(jnp.stack(parts) for parts in zip(*p_states))
    s_out = tuple(jnp.stack(parts) for parts in zip(*s_states))
    return (y_prompt, y_sample) + p_out + s_out
```

```python
import functools

import jax
import jax.numpy as jnp
from jax import lax
from jax.experimental import pallas as pl
from jax.experimental.pallas import tpu as pltpu

F32 = jnp.float32
BF16 = jnp.bfloat16

D_MODEL = 4096
HEAD_DIM = 128
W_GROUP = D_MODEL // 4
N_HEADS = W_GROUP // HEAD_DIM
GDN_CHUNK = 64
D_FF = 11008
UP_TN = 256
LRU_C = 8.0
EPS = 1e-6
SAMPLE_ROWS = 16

OFF_A_BETA = 4 * W_GROUP
OFF_B = OFF_A_BETA + 2 * N_HEADS
OFF_C_F = OFF_B + 6 * W_GROUP
OFF_D = OFF_C_F + N_HEADS
N_IN = OFF_D + 2 * W_GROUP
M_A = 0
M_B = 4 * W_GROUP
M_C = 7 * W_GROUP
M_D = 10 * W_GROUP
N_MAIN = 12 * W_GROUP
L_BETA, L_G, L_F = 0, N_HEADS, 2 * N_HEADS

VMEM_LIMIT = 56 * 1024 * 1024

NN = ((1,), (0,))
NT = ((1,), (1,))
TN = ((0,), (0,))


def _cp(*sem):
    return pltpu.CompilerParams(dimension_semantics=sem, vmem_limit_bytes=VMEM_LIMIT)


def _dg(a, b, dims=NN):
    return lax.dot_general(a, b, (dims, ((), ())), preferred_element_type=F32)


def _mm1(a, b, dims=NN):
    return _dg(a.astype(BF16), b.astype(BF16), dims)


def _split(a):
    hi = a.astype(BF16)
    lo = (a - hi.astype(F32)).astype(BF16)
    return hi, lo


def _mm3(a, b, dims=NN):
    ah, al = _split(a)
    bh, bl = _split(b)
    return _dg(ah, bh, dims) + (_dg(ah, bl, dims) + _dg(al, bh, dims))


def _mm_exact_lhs(a_bf16, b):
    b0 = b.astype(BF16)
    r1 = b - b0.astype(F32)
    b1 = r1.astype(BF16)
    b2 = (r1 - b1.astype(F32)).astype(BF16)
    return _dg(a_bf16, b0) + (_dg(a_bf16, b1) + _dg(a_bf16, b2))


def _mm_exact_rhs(a, b_bf16):
    a0 = a.astype(BF16)
    r1 = a - a0.astype(F32)
    a1 = r1.astype(BF16)
    a2 = (r1 - a1.astype(F32)).astype(BF16)
    return _dg(a0, b_bf16) + (_dg(a1, b_bf16) + _dg(a2, b_bf16))


def _silu(x):
    return x * jax.nn.sigmoid(x)


def _iota(shape, axis):
    return lax.broadcasted_iota(jnp.int32, shape, axis)


def _shift_rows(x, s):
    r = pltpu.roll(x, s, axis=0)
    return jnp.where(_iota(x.shape, 0) < s, 0.0, r)


def _lane_pick(x, lane):
    return jnp.sum(jnp.where(_iota(x.shape, 1) == lane, x, 0.0), axis=-1, keepdims=True)


def _causal_conv(x, w):
    width = w.shape[0]
    y = x * w[width - 1:width, :]
    for s in range(1, width):
        y = y + _shift_rows(x, s) * w[width - 1 - s:width - s, :]
    return y


def _ada_kernel(c_ref, w_ref, b_ref, o_ref):
    c = c_ref[...]
    o_ref[...] = _mm1(_silu(c), w_ref[...]) + b_ref[...]


def _ada(c16, w, b, layer):
    _, k, n = w.shape
    tn = 512
    return pl.pallas_call(
        _ada_kernel,
        grid=(n // tn,),
        in_specs=[pl.BlockSpec((SAMPLE_ROWS, k), lambda j: (0, 0)),
                  pl.BlockSpec((None, k, tn), lambda j: (layer, 0, j)),
                  pl.BlockSpec((None, 1, tn), lambda j: (layer, 0, j))],
        out_specs=pl.BlockSpec((SAMPLE_ROWS, tn), lambda j: (0, j)),
        out_shape=jax.ShapeDtypeStruct((SAMPLE_ROWS, n), F32),
        compiler_params=_cp("arbitrary"),
        name="ada",
    )(c16, w, b.reshape(b.shape[0], 1, n))


def _norm_mod_kernel(x_ref, w_ref, sc_ref, sh_ref, o_ref):
    x = x_ref[0]
    y = x * lax.rsqrt(jnp.mean(x * x, axis=-1, keepdims=True) + EPS) * w_ref[...]
    o_ref[0] = (y * (1.0 + sc_ref[0]) + sh_ref[0]).astype(o_ref.dtype)


def _norm_mod(x3, w, scale3, shift3, tt):
    g, r, d = x3.shape
    rm = scale3.shape[1]
    if rm == 1:
        mod_spec = pl.BlockSpec((1, 1, d), lambda b, t: (b, 0, 0))
    else:
        mod_spec = pl.BlockSpec((1, tt, d), lambda b, t: (b, t, 0))
    return pl.pallas_call(
        _norm_mod_kernel,
        grid=(g, r // tt),
        in_specs=[pl.BlockSpec((1, tt, d), lambda b, t: (b, t, 0)),
                  pl.BlockSpec((1, d), lambda b, t: (0, 0)),
                  mod_spec, mod_spec],
        out_specs=pl.BlockSpec((1, tt, d), lambda b, t: (b, t, 0)),
        out_shape=jax.ShapeDtypeStruct((g, r, d), BF16),
        compiler_params=_cp("arbitrary", "arbitrary"),
        name="norm_mod",
    )(x3, w.reshape(1, d), scale3, shift3)


def _rms_kernel(x_ref, w_ref, o_ref):
    x = x_ref[...]
    o_ref[...] = x * lax.rsqrt(jnp.mean(x * x, axis=-1, keepdims=True) + EPS) * w_ref[...]


def _rmsnorm(x2, w, tt):
    r, d = x2.shape
    return pl.pallas_call(
        _rms_kernel,
        grid=(r // tt,),
        in_specs=[pl.BlockSpec((tt, d), lambda t: (t, 0)), pl.BlockSpec((1, d), lambda t: (0, 0))],
        out_specs=pl.BlockSpec((tt, d), lambda t: (t, 0)),
        out_shape=jax.ShapeDtypeStruct((r, d), F32),
        compiler_params=_cp("arbitrary"),
        name="rmsnorm",
    )(x2, w.reshape(1, d))


def _mm_kernel(x_ref, w_ref, o_ref):
    o_ref[...] = _dg(x_ref[...], w_ref[...])


def _matmul(x, w, layer, tm, tn, name):
    m, k = x.shape
    n = w.shape[2]
    return pl.pallas_call(
        _mm_kernel,
        grid=(m // tm, n // tn),
        in_specs=[pl.BlockSpec((tm, k), lambda i, j: (i, 0)),
                  pl.BlockSpec((None, k, tn), lambda i, j: (layer, 0, j))],
        out_specs=pl.BlockSpec((tm, tn), lambda i, j: (i, j)),
        out_shape=jax.ShapeDtypeStruct((m, n), F32),
        compiler_params=_cp("arbitrary", "arbitrary"),
        name=name,
    )(x, w)


def _mm_out_kernel(a_ref, b_ref, c_ref, d_ref, w_ref, res_ref, gate_ref, o_ref):
    g = a_ref.shape[1]
    acc = _dg(a_ref[...], w_ref[0:g, :])
    acc = acc + _dg(b_ref[...], w_ref[g:2 * g, :])
    acc = acc + _dg(c_ref[...], w_ref[2 * g:3 * g, :])
    acc = acc + _dg(d_ref[...], w_ref[3 * g:4 * g, :])
    o_ref[...] = res_ref[...] + gate_ref[0] * acc


def _gate_spec(gate3, tm, tn, rows_per_group):
    if gate3.shape[1] == 1:
        return pl.BlockSpec((1, 1, tn), lambda i, j, *_: ((i * tm) // rows_per_group, 0, j))
    return pl.BlockSpec((1, tm, tn), lambda i, j, *_: (0, i, j))


def _mm_out(parts, w, layer, res, gate3, tm, tn, rows_per_group):
    m, g = parts[0].shape
    n = w.shape[2]
    part_spec = pl.BlockSpec((tm, g), lambda i, j: (i, 0))
    return pl.pallas_call(
        _mm_out_kernel,
        grid=(m // tm, n // tn),
        in_specs=[part_spec, part_spec, part_spec, part_spec,
                  pl.BlockSpec((None, 4 * g, tn), lambda i, j: (layer, 0, j)),
                  pl.BlockSpec((tm, tn), lambda i, j: (i, j)),
                  _gate_spec(gate3, tm, tn, rows_per_group)],
        out_specs=pl.BlockSpec((tm, tn), lambda i, j: (i, j)),
        out_shape=jax.ShapeDtypeStruct((m, n), F32),
        compiler_params=_cp("arbitrary", "arbitrary"),
        name="mm_out",
    )(*parts, w, res, gate3)


def _down_kernel(a_ref, w_ref, res_ref, gate_ref, o_ref):
    k = pl.program_id(2)
    part = gate_ref[0] * _dg(a_ref[...], w_ref[...])

    @pl.when(k == 0)
    def _():
        o_ref[...] = res_ref[...] + part

    @pl.when(k > 0)
    def _():
        o_ref[...] += part


def _mm_down(act, w, layer, res, gate3, tm, tn, tk, rows_per_group):
    m, kk = act.shape
    n = w.shape[2]
    return pl.pallas_call(
        _down_kernel,
        grid=(m // tm, n // tn, kk // tk),
        in_specs=[pl.BlockSpec((tm, tk), lambda i, j, k: (i, k)),
                  pl.BlockSpec((None, tk, tn), lambda i, j, k: (layer, k, j)),
                  pl.BlockSpec((tm, tn), lambda i, j, k: (i, j)),
                  _gate_spec(gate3, tm, tn, rows_per_group)],
        out_specs=pl.BlockSpec((tm, tn), lambda i, j, k: (i, j)),
        out_shape=jax.ShapeDtypeStruct((m, n), F32),
        compiler_params=_cp("arbitrary", "arbitrary", "arbitrary"),
        name="mm_down",
    )(act, w, res, gate3)


UP_TM = 2048
UP_ROW_SPLIT = 4


def _up_prompt_kernel(x_ref, wg_ref, wv_ref, cw_ref, act_ref, tail_ref, carry_ref, wb_ref,
                      *, tiles_per_seq):
    i = pl.program_id(1)
    tm = x_ref.shape[0]

    @pl.when(i == 0)
    def _():
        tn = wg_ref.shape[1]
        wb_ref[:, 0:tn] = wg_ref[...].astype(BF16)
        wb_ref[:, tn:2 * tn] = wv_ref[...].astype(BF16)

    @pl.when(i % tiles_per_seq == 0)
    def _():
        carry_ref[...] = jnp.zeros_like(carry_ref)

    prev = carry_ref[...]
    cw = cw_ref[...]
    row8 = _iota(prev.shape, 0)
    th = tm // UP_ROW_SPLIT
    for r in range(UP_ROW_SPLIT):
        x = x_ref[r * th:(r + 1) * th, :]
        gv = _dg(x, wb_ref[...])
        g = gv[:, 0:gv.shape[1] // 2]
        v = gv[:, gv.shape[1] // 2:]
        hg = g * cw[2:3, :] + _shift_rows(g, 1) * cw[1:2, :] + _shift_rows(g, 2) * cw[0:1, :]
        top = (jnp.where(row8 < 1, pltpu.roll(prev, 1, axis=0), 0.0) * cw[1:2, :]
               + jnp.where(row8 < 2, pltpu.roll(prev, 2, axis=0), 0.0) * cw[0:1, :])
        hg = jnp.concatenate([hg[0:8, :] + top, hg[8:, :]], axis=0)
        act_ref[r * th:(r + 1) * th, :] = (_silu(hg) * v).astype(act_ref.dtype)
        prev = g[th - 8:th, :]
    carry_ref[...] = prev
    tail_ref[0] = prev


def _up_prompt(h2, w_up, layer, conv_w, tm, tn, seq):
    m, k = h2.shape
    nf = w_up.shape[2] // 2
    nj = nf // tn
    kern = functools.partial(_up_prompt_kernel, tiles_per_seq=seq // tm)
    return pl.pallas_call(
        kern,
        grid=(nj, m // tm),
        in_specs=[pl.BlockSpec((tm, k), lambda j, i: (i, 0)),
                  pl.BlockSpec((None, k, tn), lambda j, i: (layer, 0, j)),
                  pl.BlockSpec((None, k, tn), lambda j, i: (layer, 0, nj + j)),
                  pl.BlockSpec((3, tn), lambda j, i: (0, j))],
        out_specs=[pl.BlockSpec((tm, tn), lambda j, i: (i, j)),
                   pl.BlockSpec((1, 8, tn), lambda j, i: ((i * tm) // seq, 0, j))],
        out_shape=[jax.ShapeDtypeStruct((m, nf), BF16),
                   jax.ShapeDtypeStruct((m // seq, 8, nf), F32)],
        scratch_shapes=[pltpu.VMEM((8, tn), F32), pltpu.VMEM((k, 2 * tn), BF16)],
        compiler_params=_cp("arbitrary", "arbitrary"),
        name="up_prompt",
    )(h2, w_up, w_up, conv_w)


def _up_sample_kernel(x_ref, wg_ref, wv_ref, cw_ref, b0_ref, b1_ref, act_ref, g_ref):
    x = x_ref[...]
    g = _dg(x, wg_ref[...].astype(BF16))
    v = _dg(x, wv_ref[...].astype(BF16))
    cw = cw_ref[...]
    hg = b0_ref[...] * cw[0:1, :] + b1_ref[...] * cw[1:2, :] + g * cw[2:3, :]
    act_ref[...] = (_silu(hg) * v).astype(act_ref.dtype)
    g_ref[...] = g


def _up_sample(h2, w_up, layer, conv_w, buf0, buf1, tn):
    m, k = h2.shape
    nf = w_up.shape[2] // 2
    nj = nf // tn
    row_spec = pl.BlockSpec((m, tn), lambda j: (0, j))
    return pl.pallas_call(
        _up_sample_kernel,
        grid=(nj,),
        in_specs=[pl.BlockSpec((m, k), lambda j: (0, 0)),
                  pl.BlockSpec((None, k, tn), lambda j: (layer, 0, j)),
                  pl.BlockSpec((None, k, tn), lambda j: (layer, 0, nj + j)),
                  pl.BlockSpec((3, tn), lambda j: (0, j)),
                  row_spec, row_spec],
        out_specs=[row_spec, row_spec],
        out_shape=[jax.ShapeDtypeStruct((m, nf), BF16), jax.ShapeDtypeStruct((m, nf), F32)],
        compiler_params=_cp("arbitrary"),
        name="up_sample",
    )(h2, w_up, w_up, conv_w, buf0, buf1)


def _head_scalars(ps, alog_row, dt_row, fb_row):
    lane = _iota(ps.shape, 1)
    beta = jax.nn.sigmoid(ps)
    g = -jnp.exp(alog_row) * jax.nn.softplus(ps + dt_row)
    logf = jax.nn.log_sigmoid(ps + fb_row)
    return jnp.where(lane < L_G, beta, jnp.where(lane < L_F, g, logf))


def _aux_kernel(ps_ref, alog_ref, dt_ref, fb_ref, a1_ref, a2_ref, ft_ref):
    t = ps_ref.shape[1]
    a1 = _head_scalars(ps_ref[0], alog_ref[...], dt_ref[...], fb_ref[...])
    a1_ref[0] = a1
    c = GDN_CHUNK
    tril = (_iota((c, c), 0) >= _iota((c, c), 1)).astype(BF16)
    keep = (_iota((1, HEAD_DIM), 1) >= L_F).astype(F32)
    carry = jnp.zeros((1, HEAD_DIM), F32)
    for ci in range(t // c):
        cs = _mm_exact_lhs(tril, a1[ci * c:(ci + 1) * c, :]) + carry * keep
        a2_ref[0, ci * c:(ci + 1) * c, :] = cs
        carry = cs[c - 1:c, :]
    a2t = a2_ref[0].T
    ft_ref[0] = a2t[L_F:L_F + N_HEADS, :]


def _aux(ps3, alog_row, dt_row, fb_row):
    b, t, n = ps3.shape
    row = pl.BlockSpec((1, n), lambda i: (0, 0))
    blk = pl.BlockSpec((1, t, n), lambda i: (i, 0, 0))
    return pl.pallas_call(
        _aux_kernel,
        grid=(b,),
        in_specs=[blk, row, row, row],
        out_specs=[blk, blk, pl.BlockSpec((1, N_HEADS, t), lambda i: (i, 0, 0))],
        out_shape=[jax.ShapeDtypeStruct((b, t, n), F32), jax.ShapeDtypeStruct((b, t, n), F32),
                   jax.ShapeDtypeStruct((b, N_HEADS, t), F32)],
        compiler_params=_cp("arbitrary"),
        name="aux",
    )(ps3, alog_row, dt_row, fb_row)


GDN_HG = 2


GDN_INV_BLOCK = 16
GDN_PREP_CHUNKS = 4


def _unit_lower_inverses(lows):
    n = lows[0].shape[0]
    bsz = GDN_INV_BLOCK
    ri = _iota((n, n), 0)
    ci = _iota((n, n), 1)
    eye = (ri == ci).astype(F32)
    nb = n // 8
    in_diag = ri // bsz == ci // bsz
    xbs = [[eye[8 * r:8 * r + 8, :] for r in range(nb)] for _ in lows]
    lbs = [[jnp.where(in_diag, low, 0.0)[8 * r:8 * r + 8, :] for r in range(nb)] for low in lows]
    for j in range(bsz - 1):
        for xb, lb in zip(xbs, lbs):
            for b in range(n // bsz):
                col = b * bsz + j
                rj = xb[col // 8][col % 8:col % 8 + 1, :]
                for r in range(col // 8, (b + 1) * bsz // 8):
                    xb[r] = xb[r] - lb[r][:, col:col + 1] * rj
    invs = [jnp.concatenate(xb, axis=0) for xb in xbs]
    s = bsz
    while s < n:
        pair = ri // (2 * s) == ci // (2 * s)
        same = ri // s == ci // s
        belows = [jnp.where(pair, jnp.where(same, 0.0, low), 0.0) for low in lows]
        left = [_mm3(inv, below) for inv, below in zip(invs, belows)]
        invs = [inv - _mm3(lt, inv) for inv, lt in zip(invs, left)]
        s *= 2
    return invs


def _gdn_prompt_kernel(q_ref, k_ref, v_ref, z_ref, wq_ref, wk_ref, wv_ref, a1_ref, a2_ref, nw_ref,
                       o_ref, s_ref, qs, ks, vs, qp, op, ap, bp, egl):
    t = q_ref.shape[0]
    c = GDN_CHUNK
    nc = t // c
    hbase = pl.program_id(1) * GDN_HG

    for h in range(GDN_HG):
        sl = slice(h * HEAD_DIM, (h + 1) * HEAD_DIM)
        q = _silu(_causal_conv(q_ref[:, sl], wq_ref[:, sl]))
        k = _silu(_causal_conv(k_ref[:, sl], wk_ref[:, sl]))
        v = _silu(_causal_conv(v_ref[:, sl], wv_ref[:, sl]))
        q = q * lax.rsqrt(jnp.sum(q * q, axis=-1, keepdims=True) + EPS) * (HEAD_DIM ** -0.5)
        k = k * lax.rsqrt(jnp.sum(k * k, axis=-1, keepdims=True) + EPS)
        qs[h] = q
        ks[h] = k
        vs[h] = v

    ri = _iota((c, c), 0)
    ci_ = _iota((c, c), 1)
    causal = ri >= ci_
    strict = ri > ci_
    eye = ri == ci_

    def prep(it, carry):
        chains = [(it * GDN_PREP_CHUNKS + cc, h) for cc in range(GDN_PREP_CHUNKS) for h in range(GDN_HG)]
        loaded = []
        for ci, h in chains:
            rows = pl.ds(pl.multiple_of(ci * c, c), c)
            loaded.append((a1_ref[0, rows, :], a2_ref[0, rows, :], qs[h, rows, :], ks[h, rows, :], vs[h, rows, :]))
        nch = range(len(chains))
        beta = [_lane_pick(loaded[i][0], L_BETA + hbase + chains[i][1]) for i in nch]
        gc = [_lane_pick(loaded[i][1], L_G + hbase + chains[i][1]) for i in nch]
        qv = [loaded[i][2] for i in nch]
        kv = [loaded[i][3] for i in nch]
        vv = [loaded[i][4] for i in nch]
        decay = []
        for i in nch:
            gc_row = jnp.sum(jnp.where(eye, gc[i], 0.0), axis=0, keepdims=True)
            decay.append(jnp.where(causal, jnp.exp(jnp.where(causal, gc[i] - gc_row, 0.0)), 0.0))
        kb = [kv[i] * beta[i] for i in nch]
        lows = [_mm1(kb[i], kv[i], NT) * jnp.where(strict, decay[i], 0.0) for i in nch]
        qk = [_mm1(qv[i], kv[i], NT) * decay[i] for i in nch]
        invs = _unit_lower_inverses(lows)
        eg = [jnp.exp(gc[i]) for i in nch]
        wu = [_mm3(invs[i], jnp.concatenate([kb[i] * eg[i], vv[i] * beta[i]], axis=1)) for i in nch]
        g_last = [gc[i][c - 1:c, :] for i in nch]
        kd = [kv[i] * jnp.exp(g_last[i] - gc[i]) for i in nch]
        kwu = [_mm1(kd[i], wu[i], TN) for i in nch]
        qwu = [_mm1(qk[i], wu[i]) for i in nch]
        results = [((qv[i] * eg[i] - qwu[i][:, :HEAD_DIM]).astype(BF16), qwu[i][:, HEAD_DIM:],
                    (-kwu[i][:, :HEAD_DIM]).astype(BF16), kwu[i][:, HEAD_DIM:],
                    jnp.broadcast_to(jnp.exp(g_last[i]), (8, HEAD_DIM))) for i in nch]
        for (ci, h), (qp_c, op_c, ap_c, bp_c, e_c) in zip(chains, results):
            rows = pl.ds(pl.multiple_of(ci * c, c), c)
            srows = pl.ds(pl.multiple_of(ci * HEAD_DIM, HEAD_DIM), HEAD_DIM)
            qp[h, rows, :] = qp_c
            op[h, rows, :] = op_c
            ap[h, srows, :] = ap_c
            bp[h, srows, :] = bp_c
            egl[h, pl.ds(pl.multiple_of(ci * 8, 8), 8), :] = e_c
        return carry

    lax.fori_loop(0, nc // GDN_PREP_CHUNKS, prep, 0)

    nw = nw_ref[...]

    def sweep(ci, states):
        rows = pl.ds(pl.multiple_of(ci * c, c), c)
        srows = pl.ds(pl.multiple_of(ci * HEAD_DIM, HEAD_DIM), HEAD_DIM)
        new_states = []
        for h in range(GDN_HG):
            s = states[h]
            sb = s.astype(BF16)
            o = _dg(qp[h, rows, :], sb) + op[h, rows, :]
            e_last = egl[h, pl.ds(pl.multiple_of(ci * 8, 8), 8), :][0:1, :]
            new_states.append(s * e_last + (_dg(ap[h, srows, :], sb) + bp[h, srows, :]))
            o = o * lax.rsqrt(jnp.mean(o * o, axis=-1, keepdims=True) + EPS) * nw
            z = z_ref[rows, h * HEAD_DIM:(h + 1) * HEAD_DIM]
            o_ref[rows, h * HEAD_DIM:(h + 1) * HEAD_DIM] = (o * _silu(z)).astype(o_ref.dtype)
        return tuple(new_states)

    init = tuple(jnp.zeros((HEAD_DIM, HEAD_DIM), F32) for _ in range(GDN_HG))
    final = lax.fori_loop(0, nc, sweep, init)
    for h in range(GDN_HG):
        s_ref[0, h] = final[h]


def _gdn_prompt(proj, a1, a2, conv_w, norm_w, b, t):
    hw = GDN_HG * HEAD_DIM
    ng = W_GROUP // hw
    col = lambda off: pl.BlockSpec((t, hw), lambda i, j: (i, off * ng + j))
    wcol = lambda off: pl.BlockSpec((4, hw), lambda i, j: (0, off * ng + j))
    aux = pl.BlockSpec((1, t, HEAD_DIM), lambda i, j: (i, 0, 0))
    nc = t // GDN_CHUNK
    scr = lambda rows, dt: pltpu.VMEM((GDN_HG, rows, HEAD_DIM), dt)
    return pl.pallas_call(
        _gdn_prompt_kernel,
        grid=(b, ng),
        in_specs=[col(0), col(1), col(2), col(3), wcol(0), wcol(1), wcol(2), aux, aux,
                  pl.BlockSpec((1, HEAD_DIM), lambda i, j: (0, 0))],
        out_specs=[pl.BlockSpec((t, hw), lambda i, j: (i, j)),
                   pl.BlockSpec((1, GDN_HG, HEAD_DIM, HEAD_DIM), lambda i, j: (i, j, 0, 0))],
        out_shape=[jax.ShapeDtypeStruct((b * t, W_GROUP), BF16),
                   jax.ShapeDtypeStruct((b, N_HEADS, HEAD_DIM, HEAD_DIM), F32)],
        scratch_shapes=[scr(t, F32), scr(t, F32), scr(t, F32), scr(t, BF16), scr(t, F32),
                        scr(nc * HEAD_DIM, BF16), scr(nc * HEAD_DIM, F32), scr(nc * 8, F32)],
        compiler_params=_cp("arbitrary", "arbitrary"),
        name="gdn_prompt",
    )(proj, proj, proj, proj, conv_w, conv_w, conv_w, a1, a2, norm_w.reshape(1, HEAD_DIM))


def _sconv_prompt_kernel(x_ref, gb_ref, gc_ref, w_ref, o_ref, tail_ref):
    u = gc_ref[...] * x_ref[...]
    t = u.shape[0]
    o_ref[...] = (gb_ref[...] * _causal_conv(u, w_ref[...])).astype(o_ref.dtype)
    tail_ref[0] = u[t - 8:t, :]


def _sconv_prompt(proj, w, b, t):
    tc = 256
    nb = W_GROUP // tc
    base = M_B // tc
    col = lambda k: pl.BlockSpec((t, tc), lambda i, j: (i, base + k * nb + j))
    return pl.pallas_call(
        _sconv_prompt_kernel,
        grid=(b, nb),
        in_specs=[col(0), col(1), col(2), pl.BlockSpec((3, tc), lambda i, j: (0, j))],
        out_specs=[pl.BlockSpec((t, tc), lambda i, j: (i, j)),
                   pl.BlockSpec((1, 8, tc), lambda i, j: (i, 0, j))],
        out_shape=[jax.ShapeDtypeStruct((b * t, W_GROUP), BF16),
                   jax.ShapeDtypeStruct((b, 8, W_GROUP), F32)],
        compiler_params=_cp("arbitrary", "arbitrary"),
        name="sconv_prompt",
    )(proj, proj, proj, w)


FOX_TQ = 512


def _fox_prompt_kernel(q_ref, k_ref, v_ref, ft_ref, o_ref):
    h = pl.program_id(1)
    qi = pl.program_id(2)
    q = (q_ref[...] * (HEAD_DIM ** -0.5)).astype(BF16)
    tq = q.shape[0]
    tk = tq
    hsel = _iota((N_HEADS, tk), 0) == h

    def tile(j, carry, diagonal):
        m, l, acc = carry
        start = pl.multiple_of(j * tk, tk)
        k = k_ref[pl.ds(start, tk), :].astype(BF16)
        v = v_ref[pl.ds(start, tk), :].astype(BF16)
        fk = jnp.sum(jnp.where(hsel, ft_ref[0, :, pl.ds(start, tk)], 0.0), axis=0, keepdims=True)
        s = _dg(q, k, NT) - fk
        if diagonal:
            s = jnp.where(_iota((tq, tk), 1) <= _iota((tq, tk), 0), s, -jnp.inf)
        m_new = jnp.maximum(m, jnp.max(s, axis=-1, keepdims=True))
        alpha = jnp.exp(m - m_new)
        p = jnp.exp(s - m_new)
        l = alpha * l + jnp.sum(p, axis=-1, keepdims=True)
        acc = alpha * acc + _dg(p.astype(BF16), v)
        return m_new, l, acc

    init = (jnp.full((tq, 1), -jnp.inf, F32), jnp.zeros((tq, 1), F32), jnp.zeros((tq, HEAD_DIM), F32))
    carry = lax.fori_loop(0, qi, lambda j, c: tile(j, c, False), init)
    _, l, acc = tile(qi, carry, True)
    o_ref[...] = (acc / l).astype(o_ref.dtype)


def _fox_prompt(proj, ft, b, t):
    tq = min(FOX_TQ, t)
    nq = t // tq
    base = M_C // HEAD_DIM
    return pl.pallas_call(
        _fox_prompt_kernel,
        grid=(b, N_HEADS, nq),
        in_specs=[pl.BlockSpec((tq, HEAD_DIM), lambda i, h, q: (i * nq + q, base + h)),
                  pl.BlockSpec((t, HEAD_DIM), lambda i, h, q: (i, base + N_HEADS + h)),
                  pl.BlockSpec((t, HEAD_DIM), lambda i, h, q: (i, base + 2 * N_HEADS + h)),
                  pl.BlockSpec((1, N_HEADS, t), lambda i, h, q: (i, 0, 0))],
        out_specs=pl.BlockSpec((tq, HEAD_DIM), lambda i, h, q: (i * nq + q, h)),
        out_shape=jax.ShapeDtypeStruct((b * t, W_GROUP), BF16),
        compiler_params=_cp("arbitrary", "arbitrary", "arbitrary"),
        name="fox_prompt",
    )(proj, proj, proj, ft)


def _lru_gates(xd, wr, br, wi, bi, lam):
    r = jax.nn.sigmoid(_mm3(xd, wr) + br)
    i = jax.nn.sigmoid(_mm3(xd, wi) + bi)
    log_a = -LRU_C * r * jax.nn.softplus(-lam)
    a = jnp.exp(log_a)
    th = jnp.tanh(log_a)
    one_minus_a2 = -2.0 * th / (1.0 - th)
    return a, jnp.sqrt(one_minus_a2) * (i * xd)


def _lru_prompt_kernel(x_ref, y_ref, cw_ref, cb_ref, wr_ref, br_ref, wi_ref, bi_ref, lam_ref,
                       o_ref, tail_ref):
    t = x_ref.shape[0]
    xd = _causal_conv(x_ref[...], cw_ref[...]) + cb_ref[...]
    a, bb = _lru_gates(xd, wr_ref[0], br_ref[...], wi_ref[0], bi_ref[...], lam_ref[...])
    row = _iota(a.shape, 0)
    s = 1
    while s < t:
        a_s = jnp.where(row < s, 1.0, pltpu.roll(a, s, axis=0))
        b_s = jnp.where(row < s, 0.0, pltpu.roll(bb, s, axis=0))
        bb = a * b_s + bb
        a = a * a_s
        s *= 2
    o_ref[...] = (jax.nn.gelu(y_ref[...]) * bb).astype(o_ref.dtype)
    tail_ref[0] = bb[t - 8:t, :]


def _lru_prompt(proj, cw, cb, wr, br, wi, bi, lam, b, t):
    n = HEAD_DIM
    nb = W_GROUP // n
    base = M_D // n
    vec = lambda: pl.BlockSpec((1, n), lambda i, j: (0, j))
    mat = lambda: pl.BlockSpec((1, n, n), lambda i, j: (j, 0, 0))
    r1 = lambda a: a.reshape(1, W_GROUP)
    return pl.pallas_call(
        _lru_prompt_kernel,
        grid=(b, nb),
        in_specs=[pl.BlockSpec((t, n), lambda i, j: (i, base + j)),
                  pl.BlockSpec((t, n), lambda i, j: (i, base + nb + j)),
                  pl.BlockSpec((4, n), lambda i, j: (0, j)),
                  vec(), mat(), vec(), mat(), vec(), vec()],
        out_specs=[pl.BlockSpec((t, n), lambda i, j: (i, j)),
                   pl.BlockSpec((1, 8, n), lambda i, j: (i, 0, j))],
        out_shape=[jax.ShapeDtypeStruct((b * t, W_GROUP), BF16),
                   jax.ShapeDtypeStruct((b, 8, W_GROUP), F32)],
        compiler_params=_cp("arbitrary", "arbitrary"),
        name="lru_prompt",
    )(proj, proj, cw, r1(cb), wr, r1(br), wi, r1(bi), r1(lam))


def _gdn_sample_kernel(q_ref, k_ref, v_ref, z_ref, b0_ref, b1_ref, b2_ref, cw_ref, ps_ref,
                       alog_ref, dt_ref, fb_ref, nw_ref, s_ref, o_ref, so_ref, hs_ref):
    i = pl.program_id(0)
    cw = cw_ref[...]
    hs = _head_scalars(ps_ref[...], alog_ref[...], dt_ref[...], fb_ref[...])
    hs_ref[...] = hs
    hs_i = jnp.sum(jnp.where(_iota(hs.shape, 0) == i, hs, 0.0), axis=0, keepdims=True)
    row_i = lambda ref: ref[pl.ds(i, 1), :]
    b0 = row_i(b0_ref)
    b1 = row_i(b1_ref)
    b2 = row_i(b2_ref)
    eye = _iota((HEAD_DIM, HEAD_DIM), 0) == _iota((HEAD_DIM, HEAD_DIM), 1)
    nw = nw_ref[...]

    def conv(x_row, off):
        w = cw[:, off:off + W_GROUP]
        bo = slice(off, off + W_GROUP)
        return _silu(b0[:, bo] * w[0:1, :] + b1[:, bo] * w[1:2, :] + b2[:, bo] * w[2:3, :] + x_row * w[3:4, :])

    q_all = conv(row_i(q_ref), 0)
    k_all = conv(row_i(k_ref), W_GROUP)
    v_all = conv(row_i(v_ref), 2 * W_GROUP)
    z_all = row_i(z_ref)
    for h in range(N_HEADS):
        sl = slice(h * HEAD_DIM, (h + 1) * HEAD_DIM)
        q = q_all[:, sl]
        k = k_all[:, sl]
        v = v_all[:, sl]
        q = q * lax.rsqrt(jnp.sum(q * q, axis=-1, keepdims=True) + EPS) * (HEAD_DIM ** -0.5)
        k = k * lax.rsqrt(jnp.sum(k * k, axis=-1, keepdims=True) + EPS)
        beta = hs_i[:, L_BETA + h:L_BETA + h + 1]
        g = hs_i[:, L_G + h:L_G + h + 1]
        k_col = jnp.sum(jnp.where(eye, k, 0.0), axis=1, keepdims=True)
        q_col = jnp.sum(jnp.where(eye, q, 0.0), axis=1, keepdims=True)
        s = s_ref[0, h] * jnp.exp(g)
        delta = (v - jnp.sum(k_col * s, axis=0, keepdims=True)) * beta
        s = s + k_col * delta
        so_ref[0, h] = s
        o = jnp.sum(q_col * s, axis=0, keepdims=True)
        o = o * lax.rsqrt(jnp.mean(o * o, axis=-1, keepdims=True) + EPS) * nw
        o_ref[0, :, sl] = o * _silu(z_all[:, sl])


def _gdn_sample(proj, ps, buf, conv_w, alog_row, dt_row, fb_row, norm_w, state, bs):
    rows = proj.shape[0]
    col = lambda k: pl.BlockSpec((rows, W_GROUP), lambda i: (0, k))
    full = lambda a: pl.BlockSpec(a.shape, lambda i: (0,) * a.ndim)
    prow = pl.BlockSpec((1, HEAD_DIM), lambda i: (0, 0))
    sblk = pl.BlockSpec((1, N_HEADS, HEAD_DIM, HEAD_DIM), lambda i: (i, 0, 0, 0))
    return pl.pallas_call(
        _gdn_sample_kernel,
        grid=(bs,),
        in_specs=[col(0), col(1), col(2), col(3), full(buf[0]), full(buf[1]), full(buf[2]),
                  full(conv_w), pl.BlockSpec((rows, HEAD_DIM), lambda i: (0, 0)), prow, prow, prow, prow, sblk],
        out_specs=[pl.BlockSpec((1, 1, W_GROUP), lambda i: (i, 0, 0)), sblk,
                   pl.BlockSpec((rows, HEAD_DIM), lambda i: (0, 0))],
        out_shape=[jax.ShapeDtypeStruct((bs, 1, W_GROUP), F32),
                   jax.ShapeDtypeStruct(state.shape, F32),
                   jax.ShapeDtypeStruct((rows, HEAD_DIM), F32)],
        compiler_params=_cp("arbitrary"),
        name="gdn_sample",
    )(proj, proj, proj, proj, buf[0], buf[1], buf[2], conv_w, ps, alog_row, dt_row, fb_row,
      norm_w.reshape(1, HEAD_DIM), state)


def _bd_sample_kernel(xb_ref, gb_ref, gc_ref, sb0_ref, sb1_ref, sw_ref,
                      xd_ref, yd_ref, lb0_ref, lb1_ref, lb2_ref, lh_ref, cw_ref, cb_ref,
                      wr_ref, br_ref, wi_ref, bi_ref, lam_ref,
                      ob_ref, u_ref, od_ref, h_ref):
    sw = sw_ref[...]
    u = gc_ref[...] * xb_ref[...]
    ob_ref[...] = gb_ref[...] * (sb0_ref[...] * sw[0:1, :] + sb1_ref[...] * sw[1:2, :] + u * sw[2:3, :])
    u_ref[...] = u
    cw = cw_ref[...]
    xd = (lb0_ref[...] * cw[0:1, :] + lb1_ref[...] * cw[1:2, :] + lb2_ref[...] * cw[2:3, :]
          + xd_ref[...] * cw[3:4, :] + cb_ref[...])
    n = HEAD_DIM
    for j in range(W_GROUP // n):
        sl = slice(j * n, (j + 1) * n)
        a, bb = _lru_gates(xd[:, sl], wr_ref[j], br_ref[:, sl], wi_ref[j], bi_ref[:, sl], lam_ref[:, sl])
        hn = a * lh_ref[:, sl] + bb
        h_ref[:, sl] = hn
        od_ref[:, sl] = jax.nn.gelu(yd_ref[:, sl]) * hn


def _bd_sample(proj, sbuf, sw, lbuf, lh, cw, cb, wr, br, wi, bi, lam):
    rows = proj.shape[0]
    r1 = lambda a: a.reshape(1, W_GROUP)
    pcol = lambda c: pl.BlockSpec((rows, W_GROUP), lambda i: (0, c // W_GROUP))
    full = lambda a: pl.BlockSpec(a.shape, lambda i: (0,) * a.ndim)
    ins = [proj, proj, proj, sbuf[0], sbuf[1], sw, proj, proj, lbuf[0], lbuf[1], lbuf[2], lh, cw,
           r1(cb), wr, r1(br), wi, r1(bi), r1(lam)]
    specs = [pcol(M_B), pcol(M_B + W_GROUP), pcol(M_B + 2 * W_GROUP)] + [full(a) for a in ins[3:6]] \
        + [pcol(M_D), pcol(M_D + W_GROUP)] + [full(a) for a in ins[8:]]
    out = jax.ShapeDtypeStruct((rows, W_GROUP), F32)
    ospec = pl.BlockSpec((rows, W_GROUP), lambda i: (0, 0))
    return pl.pallas_call(
        _bd_sample_kernel,
        grid=(1,),
        in_specs=specs,
        out_specs=[ospec] * 4,
        out_shape=[out] * 4,
        compiler_params=_cp("arbitrary"),
        name="bd_sample",
    )(*ins)


FOX_PAGES_PER_STEP = 8


def _fox_sample_kernel(pt_ref, q_ref, kn_ref, vn_ref, r0_ref, *refs):
    del pt_ref
    n = FOX_PAGES_PER_STEP
    kp_refs, vp_refs, lf_refs = refs[0:n], refs[n:2 * n], refs[2 * n:3 * n]
    o_ref, m_sc, l_sc, acc_sc, r_sc = refs[3 * n:]
    i = pl.program_id(0)
    p = pl.program_id(1)
    npg = pl.num_programs(1)
    scale = HEAD_DIM ** -0.5
    q = q_ref[i]

    @pl.when(p == 0)
    def _():
        m_sc[...] = jnp.sum(q * kn_ref[i], axis=-1, keepdims=True) * scale
        l_sc[...] = jnp.ones_like(l_sc)
        acc_sc[...] = vn_ref[i]
        r_sc[...] = r0_ref[pl.ds(i, 1), :]

    m = m_sc[...]
    l = l_sc[...]
    acc = acc_sc[...]
    r = r_sc[...]
    pg, nh, hd = kp_refs[0].shape
    flat = pg * nh
    qb = q.astype(BF16)
    lane = _iota((1, flat), 1)
    own = _iota((nh, flat), 1) % nh == _iota((nh, flat), 0)
    pages = range(n)
    lfs = [lf_refs[t][...] for t in pages]
    suffix = list(lfs)
    total = list(lfs)
    step = nh
    while step < flat:
        suffix = [x + jnp.where(lane + step < flat, pltpu.roll(x, flat - step, axis=1), 0.0) for x in suffix]
        total = [x + pltpu.roll(x, step, axis=1) for x in total]
        step *= 2
    scores = [_dg(qb, kp_refs[t][...].reshape(flat, hd).astype(BF16), NT) * scale for t in pages]
    s = []
    for t in pages:
        s.append(jnp.where(own, scores[t] + ((suffix[t] - lfs[t]) + r), -jnp.inf))
        r = r + total[t]
    m_new = m
    for t in pages:
        m_new = jnp.maximum(m_new, jnp.max(s[t], axis=-1, keepdims=True))
    alpha = jnp.exp(m - m_new)
    pw = [jnp.exp(s[t] - m_new) for t in pages]
    l = alpha * l
    acc = alpha * acc
    for t in pages:
        l = l + jnp.sum(pw[t], axis=-1, keepdims=True)
        acc = acc + _dg(pw[t].astype(BF16), vp_refs[t][...].reshape(flat, hd).astype(BF16))
    m_sc[...] = m_new
    l_sc[...] = l
    acc_sc[...] = acc
    r_sc[...] = r

    @pl.when(p == npg - 1)
    def _():
        o_ref[0] = acc / l


def _fox_sample(q3, kn3, vn3, logf_new, page_table, ck, cv, clf, layer, bs):
    rows = q3.shape[0]
    n_pages = page_table.shape[1]
    page = ck.shape[2]
    flat = page * N_HEADS
    n = FOX_PAGES_PER_STEP
    assert n_pages % n == 0
    r0 = jnp.tile(logf_new, (1, page))
    clf_flat = clf.reshape(clf.shape[0], clf.shape[1], 1, flat)
    new_spec = pl.BlockSpec((rows, N_HEADS, HEAD_DIM), lambda i, p, pt: (0, 0, 0))

    def kv_spec(t):
        return pl.BlockSpec((None, None, page, N_HEADS, HEAD_DIM),
                            lambda i, p, pt: (layer, pt[i, n_pages - 1 - (p * n + t)], 0, 0, 0))

    def lf_spec(t):
        return pl.BlockSpec((None, None, 1, flat),
                            lambda i, p, pt: (layer, pt[i, n_pages - 1 - (p * n + t)], 0, 0))

    slots = list(range(n))
    grid_spec = pltpu.PrefetchScalarGridSpec(
        num_scalar_prefetch=1,
        grid=(bs, n_pages // n),
        in_specs=[new_spec, new_spec, new_spec, pl.BlockSpec((rows, flat), lambda i, p, pt: (0, 0))]
        + [kv_spec(t) for t in slots] + [kv_spec(t) for t in slots] + [lf_spec(t) for t in slots],
        out_specs=pl.BlockSpec((1, N_HEADS, HEAD_DIM), lambda i, p, pt: (i, 0, 0)),
        scratch_shapes=[pltpu.VMEM((N_HEADS, 1), F32), pltpu.VMEM((N_HEADS, 1), F32),
                        pltpu.VMEM((N_HEADS, HEAD_DIM), F32), pltpu.VMEM((1, flat), F32)],
    )
    return pl.pallas_call(
        _fox_sample_kernel,
        grid_spec=grid_spec,
        out_shape=jax.ShapeDtypeStruct((bs, N_HEADS, HEAD_DIM), F32),
        compiler_params=_cp("arbitrary", "arbitrary"),
        name="fox_sample",
    )(page_table, q3, kn3, vn3, r0, *([ck] * n), *([cv] * n), *([clf_flat] * n))


def _prep_weights(w_in, w_out, w_up, w_down):
    w_main = jnp.concatenate([w_in[:, :, 0:OFF_A_BETA], w_in[:, :, OFF_B:OFF_C_F], w_in[:, :, OFF_D:N_IN]],
                             axis=2).astype(BF16)
    w_small = jnp.concatenate([w_in[:, :, OFF_A_BETA:OFF_B], w_in[:, :, OFF_C_F:OFF_D],
                               jnp.zeros(w_in.shape[:2] + (HEAD_DIM - 3 * N_HEADS,), w_in.dtype)],
                              axis=2).astype(BF16)
    return dict(w_main=w_main, w_small=w_small, w_out=w_out.astype(BF16), w_up=w_up,
                w_down=w_down.astype(BF16))


def _lane_row(v, off):
    return jnp.zeros((1, HEAD_DIM), F32).at[0, off:off + N_HEADS].set(v)


def _pad_rows(a, rows):
    return jnp.pad(a, ((0, rows - a.shape[0]),) + ((0, 0),) * (a.ndim - 1))


def kernel(x_prompt, x_sample, cache_k, cache_v, cache_logf, state_gdn, state_gdn_conv, state_sconv,
           state_lru_h, state_lru_conv, state_ffn_conv, page_table, c_prompt, c_sample, ada_w, ada_b,
           ln1_w, ln2_w, w_in, gdn_conv_w, gdn_a_log, gdn_dt_bias, gdn_norm_w, sconv_w, fox_f_bias,
           lru_conv_w, lru_conv_b, lru_wr, lru_br, lru_wi, lru_bi, lru_lambda, w_out, ffn_conv_w,
           w_up, w_down, final_norm_w):
    bp, t, d = x_prompt.shape
    bs = x_sample.shape[0]
    depth = w_in.shape[0]
    sr = SAMPLE_ROWS
    tm = min(1024, t)
    tnorm = min(512, t)
    c16 = _pad_rows(jnp.concatenate([c_prompt, c_sample], axis=0), sr)

    xp = x_prompt.reshape(bp * t, d)
    xs = _pad_rows(x_sample.reshape(bs, d), sr)
    p_states, s_states = [], []
    w = _prep_weights(w_in, w_out, w_up, w_down)
    for l in range(depth):
        alog_row = _lane_row(gdn_a_log[l], L_G)
        dt_row = _lane_row(gdn_dt_bias[l], L_G)
        fb_row = _lane_row(fox_f_bias[l], L_F)
        ffn_cw = ffn_conv_w[l]
        mod = _ada(c16, ada_w, ada_b, l).reshape(sr, 6, d)
        pm = [mod[0:bp, k][:, None, :] for k in range(6)]
        sm = [_pad_rows(mod[bp:bp + bs, k], sr)[None] for k in range(6)]

        h = _norm_mod(xp.reshape(bp, t, d), ln1_w[l], pm[1], pm[0], tnorm).reshape(bp * t, d)
        proj = _matmul(h, w["w_main"], l, tm, 1024, "mm_in")
        ps = _matmul(h, w["w_small"], l, tm, HEAD_DIM, "mm_in_small")
        a1, a2, ft = _aux(ps.reshape(bp, t, HEAD_DIM), alog_row, dt_row, fb_row)
        oa, p_gdn = _gdn_prompt(proj, a1, a2, gdn_conv_w[l], gdn_norm_w[l], bp, t)
        ob, sconv_tail = _sconv_prompt(proj, sconv_w[l], bp, t)
        oc = _fox_prompt(proj, ft, bp, t)
        od, lru_tail = _lru_prompt(proj, lru_conv_w[l], lru_conv_b[l], lru_wr[l], lru_br[l], lru_wi[l],
                                   lru_bi[l], lru_lambda[l], bp, t)
        xp = _mm_out([oa, ob, oc, od], w["w_out"], l, xp, pm[2], tm, 1024, t)
        h2 = _norm_mod(xp.reshape(bp, t, d), ln2_w[l], pm[4], pm[3], tnorm).reshape(bp * t, d)
        act, ffn_tail = _up_prompt(h2, w["w_up"], l, ffn_cw, min(UP_TM, t), UP_TN, t)
        xp = _mm_down(act, w["w_down"], l, xp, pm[5], min(512, t), 512, D_FF, t)

        proj3 = proj.reshape(bp, t, N_MAIN)
        p_states.append((
            proj3[:, :, M_C + W_GROUP:M_C + 2 * W_GROUP].reshape(bp, t, N_HEADS, HEAD_DIM),
            proj3[:, :, M_C + 2 * W_GROUP:M_C + 3 * W_GROUP].reshape(bp, t, N_HEADS, HEAD_DIM),
            a1[:, :, L_F:L_F + N_HEADS],
            p_gdn,
            proj3[:, t - 3:, M_A:M_A + 3 * W_GROUP],
            sconv_tail[:, 6:8, :],
            lru_tail[:, 7, :],
            proj3[:, t - 3:, M_D:M_D + W_GROUP],
            ffn_tail[:, 6:8, :],
        ))

        hs_in = _norm_mod(xs[None], ln1_w[l], sm[1], sm[0], sr)[0]
        sproj = _matmul(hs_in, w["w_main"], l, sr, 2048, "mm_in_s")
        sps = _matmul(hs_in, w["w_small"], l, sr, HEAD_DIM, "mm_in_small_s")
        gbuf = [_pad_rows(state_gdn_conv[l][:, j, :], sr) for j in range(3)]
        soa, s_gdn, shs = _gdn_sample(sproj, sps, gbuf, gdn_conv_w[l], alog_row, dt_row,
                                      fb_row, gdn_norm_w[l], state_gdn[l], bs)
        sbuf = [_pad_rows(state_sconv[l][:, j, :], sr) for j in range(2)]
        lbuf = [_pad_rows(state_lru_conv[l][:, j, :], sr) for j in range(3)]
        sob, s_u, sod, s_h = _bd_sample(sproj, sbuf, sconv_w[l], lbuf, _pad_rows(state_lru_h[l], sr),
                                        lru_conv_w[l], lru_conv_b[l], lru_wr[l], lru_br[l], lru_wi[l],
                                        lru_bi[l], lru_lambda[l])
        heads3 = lambda off: sproj[:, off:off + W_GROUP].reshape(sr, N_HEADS, HEAD_DIM)
        soc = _fox_sample(heads3(M_C), heads3(M_C + W_GROUP), heads3(M_C + 2 * W_GROUP),
                          shs[:, L_F:L_F + N_HEADS], page_table, cache_k, cache_v, cache_logf, l, bs)
        parts = [_pad_rows(soa.reshape(bs, W_GROUP), sr).astype(BF16), sob.astype(BF16),
                 _pad_rows(soc.reshape(bs, W_GROUP), sr).astype(BF16), sod.astype(BF16)]
        xs = _mm_out(parts, w["w_out"], l, xs, sm[2], sr, 2048, sr)
        sh2 = _norm_mod(xs[None], ln2_w[l], sm[4], sm[3], sr)[0]
        fbuf = [_pad_rows(state_ffn_conv[l][:, j, :], sr) for j in range(2)]
        sact, s_g = _up_sample(sh2, w["w_up"], l, ffn_cw, fbuf[0], fbuf[1], UP_TN)
        xs = _mm_down(sact, w["w_down"], l, xs, sm[5], sr, 512, D_FF, sr)

        sp = sproj[:bs]
        s_states.append((
            sp[:, M_C + W_GROUP:M_C + 2 * W_GROUP].reshape(bs, 1, N_HEADS, HEAD_DIM),
            sp[:, M_C + 2 * W_GROUP:M_C + 3 * W_GROUP].reshape(bs, 1, N_HEADS, HEAD_DIM),
            shs[:bs, L_F:L_F + N_HEADS].reshape(bs, 1, N_HEADS),
            s_gdn,
            jnp.concatenate([state_gdn_conv[l][:, 1:, :], sp[:, None, M_A:M_A + 3 * W_GROUP]], axis=1),
            jnp.concatenate([state_sconv[l][:, 1:, :], s_u[:bs, None, :]], axis=1),
            s_h[:bs],
            jnp.concatenate([state_lru_conv[l][:, 1:, :], sp[:, None, M_D:M_D + W_GROUP]], axis=1),
            jnp.concatenate([state_ffn_conv[l][:, 1:, :], s_g[:bs, None, :]], axis=1),
        ))

    y_prompt = _rmsnorm(xp, final_norm_w, tnorm).reshape(bp, t, d)
    y_sample = _rmsnorm(xs, final_norm_w, sr)[:bs].reshape(bs, 1, d)
    p_out = tuple(jnp.stack(parts) for parts in zip(*p_states))
    s_out = tuple(jnp.stack(parts) for parts in zip(*s_states))
    return (y_prompt, y_sample) + p_out + s_out
```

```python
import functools

import jax
import jax.numpy as jnp
from jax import lax
from jax.experimental import pallas as pl
from jax.experimental.pallas import tpu as pltpu

F32 = jnp.float32
BF16 = jnp.bfloat16

D_MODEL = 4096
HEAD_DIM = 128
W_GROUP = D_MODEL // 4
N_HEADS = W_GROUP // HEAD_DIM
GDN_CHUNK = 64
D_FF = 11008
UP_TN = 256
LRU_C = 8.0
EPS = 1e-6
SAMPLE_ROWS = 16

OFF_A_BETA = 4 * W_GROUP
OFF_B = OFF_A_BETA + 2 * N_HEADS
OFF_C_F = OFF_B + 6 * W_GROUP
OFF_D = OFF_C_F + N_HEADS
N_IN = OFF_D + 2 * W_GROUP
M_A = 0
M_B = 4 * W_GROUP
M_C = 7 * W_GROUP
M_D = 10 * W_GROUP
N_MAIN = 12 * W_GROUP
L_BETA, L_G, L_F = 0, N_HEADS, 2 * N_HEADS

VMEM_LIMIT = 56 * 1024 * 1024

NN = ((1,), (0,))
NT = ((1,), (1,))
TN = ((0,), (0,))


def _cp(*sem):
    return pltpu.CompilerParams(dimension_semantics=sem, vmem_limit_bytes=VMEM_LIMIT)


def _dg(a, b, dims=NN):
    return lax.dot_general(a, b, (dims, ((), ())), preferred_element_type=F32)


def _mm1(a, b, dims=NN):
    return _dg(a.astype(BF16), b.astype(BF16), dims)


def _split(a):
    hi = a.astype(BF16)
    lo = (a - hi.astype(F32)).astype(BF16)
    return hi, lo


def _mm3(a, b, dims=NN):
    ah, al = _split(a)
    bh, bl = _split(b)
    return _dg(ah, bh, dims) + (_dg(ah, bl, dims) + _dg(al, bh, dims))


def _mm_exact_lhs(a_bf16, b):
    b0 = b.astype(BF16)
    r1 = b - b0.astype(F32)
    b1 = r1.astype(BF16)
    b2 = (r1 - b1.astype(F32)).astype(BF16)
    return _dg(a_bf16, b0) + (_dg(a_bf16, b1) + _dg(a_bf16, b2))


def _mm_exact_rhs(a, b_bf16):
    a0 = a.astype(BF16)
    r1 = a - a0.astype(F32)
    a1 = r1.astype(BF16)
    a2 = (r1 - a1.astype(F32)).astype(BF16)
    return _dg(a0, b_bf16) + (_dg(a1, b_bf16) + _dg(a2, b_bf16))


def _silu(x):
    return x * jax.nn.sigmoid(x)


def _iota(shape, axis):
    return lax.broadcasted_iota(jnp.int32, shape, axis)


def _shift_rows(x, s):
    r = pltpu.roll(x, s, axis=0)
    return jnp.where(_iota(x.shape, 0) < s, 0.0, r)


def _lane_pick(x, lane):
    return jnp.sum(jnp.where(_iota(x.shape, 1) == lane, x, 0.0), axis=-1, keepdims=True)


def _causal_conv(x, w):
    width = w.shape[0]
    y = x * w[width - 1:width, :]
    for s in range(1, width):
        y = y + _shift_rows(x, s) * w[width - 1 - s:width - s, :]
    return y


ADA_TN = 1024


def _ada_kernel(c_ref, w_ref, b_ref, o_ref):
    c = c_ref[...]
    o_ref[...] = _mm1(_silu(c), w_ref[...]) + b_ref[...]


def _ada(c16, w, b, layer):
    _, k, n = w.shape
    tn = ADA_TN
    return pl.pallas_call(
        _ada_kernel,
        grid=(n // tn,),
        in_specs=[pl.BlockSpec((SAMPLE_ROWS, k), lambda j: (0, 0)),
                  pl.BlockSpec((None, k, tn), lambda j: (layer, 0, j)),
                  pl.BlockSpec((None, 1, tn), lambda j: (layer, 0, j))],
        out_specs=pl.BlockSpec((SAMPLE_ROWS, tn), lambda j: (0, j)),
        out_shape=jax.ShapeDtypeStruct((SAMPLE_ROWS, n), F32),
        compiler_params=_cp("arbitrary"),
        name="ada",
    )(c16, w, b.reshape(b.shape[0], 1, n))


def _norm_mod_kernel(x_ref, w_ref, sc_ref, sh_ref, o_ref):
    x = x_ref[0]
    y = x * lax.rsqrt(jnp.mean(x * x, axis=-1, keepdims=True) + EPS) * w_ref[...]
    o_ref[0] = (y * (1.0 + sc_ref[0]) + sh_ref[0]).astype(o_ref.dtype)


def _norm_mod(x3, w, scale3, shift3, tt):
    g, r, d = x3.shape
    rm = scale3.shape[1]
    if rm == 1:
        mod_spec = pl.BlockSpec((1, 1, d), lambda b, t: (b, 0, 0))
    else:
        mod_spec = pl.BlockSpec((1, tt, d), lambda b, t: (b, t, 0))
    return pl.pallas_call(
        _norm_mod_kernel,
        grid=(g, r // tt),
        in_specs=[pl.BlockSpec((1, tt, d), lambda b, t: (b, t, 0)),
                  pl.BlockSpec((1, d), lambda b, t: (0, 0)),
                  mod_spec, mod_spec],
        out_specs=pl.BlockSpec((1, tt, d), lambda b, t: (b, t, 0)),
        out_shape=jax.ShapeDtypeStruct((g, r, d), BF16),
        compiler_params=_cp("arbitrary", "arbitrary"),
        name="norm_mod",
    )(x3, w.reshape(1, d), scale3, shift3)


def _rms_kernel(x_ref, w_ref, o_ref):
    x = x_ref[...]
    o_ref[...] = x * lax.rsqrt(jnp.mean(x * x, axis=-1, keepdims=True) + EPS) * w_ref[...]


def _rmsnorm(x2, w, tt):
    r, d = x2.shape
    return pl.pallas_call(
        _rms_kernel,
        grid=(r // tt,),
        in_specs=[pl.BlockSpec((tt, d), lambda t: (t, 0)), pl.BlockSpec((1, d), lambda t: (0, 0))],
        out_specs=pl.BlockSpec((tt, d), lambda t: (t, 0)),
        out_shape=jax.ShapeDtypeStruct((r, d), F32),
        compiler_params=_cp("arbitrary"),
        name="rmsnorm",
    )(x2, w.reshape(1, d))


def _mm_kernel(x_ref, w_ref, o_ref):
    o_ref[...] = _dg(x_ref[...], w_ref[...])


def _matmul(x, w, layer, tm, tn, name):
    m, k = x.shape
    n = w.shape[2]
    return pl.pallas_call(
        _mm_kernel,
        grid=(m // tm, n // tn),
        in_specs=[pl.BlockSpec((tm, k), lambda i, j: (i, 0)),
                  pl.BlockSpec((None, k, tn), lambda i, j: (layer, 0, j))],
        out_specs=pl.BlockSpec((tm, tn), lambda i, j: (i, j)),
        out_shape=jax.ShapeDtypeStruct((m, n), F32),
        compiler_params=_cp("arbitrary", "arbitrary"),
        name=name,
    )(x, w)


def _mm_out_kernel(a_ref, b_ref, c_ref, d_ref, w_ref, res_ref, gate_ref, o_ref):
    g = a_ref.shape[1]
    acc = _dg(a_ref[...], w_ref[0:g, :])
    acc = acc + _dg(b_ref[...], w_ref[g:2 * g, :])
    acc = acc + _dg(c_ref[...], w_ref[2 * g:3 * g, :])
    acc = acc + _dg(d_ref[...], w_ref[3 * g:4 * g, :])
    o_ref[...] = res_ref[...] + gate_ref[0] * acc


def _gate_spec(gate3, tm, tn, rows_per_group):
    if gate3.shape[1] == 1:
        return pl.BlockSpec((1, 1, tn), lambda i, j, *_: ((i * tm) // rows_per_group, 0, j))
    return pl.BlockSpec((1, tm, tn), lambda i, j, *_: (0, i, j))


def _mm_out(parts, w, layer, res, gate3, tm, tn, rows_per_group):
    m, g = parts[0].shape
    n = w.shape[2]
    part_spec = pl.BlockSpec((tm, g), lambda i, j: (i, 0))
    return pl.pallas_call(
        _mm_out_kernel,
        grid=(m // tm, n // tn),
        in_specs=[part_spec, part_spec, part_spec, part_spec,
                  pl.BlockSpec((None, 4 * g, tn), lambda i, j: (layer, 0, j)),
                  pl.BlockSpec((tm, tn), lambda i, j: (i, j)),
                  _gate_spec(gate3, tm, tn, rows_per_group)],
        out_specs=pl.BlockSpec((tm, tn), lambda i, j: (i, j)),
        out_shape=jax.ShapeDtypeStruct((m, n), F32),
        compiler_params=_cp("arbitrary", "arbitrary"),
        name="mm_out",
    )(*parts, w, res, gate3)


def _down_kernel(a_ref, w_ref, res_ref, gate_ref, o_ref):
    k = pl.program_id(2)
    part = gate_ref[0] * _dg(a_ref[...], w_ref[...])

    @pl.when(k == 0)
    def _():
        o_ref[...] = res_ref[...] + part

    @pl.when(k > 0)
    def _():
        o_ref[...] += part


def _mm_down(act, w, layer, res, gate3, tm, tn, tk, rows_per_group):
    m, kk = act.shape
    n = w.shape[2]
    return pl.pallas_call(
        _down_kernel,
        grid=(m // tm, n // tn, kk // tk),
        in_specs=[pl.BlockSpec((tm, tk), lambda i, j, k: (i, k)),
                  pl.BlockSpec((None, tk, tn), lambda i, j, k: (layer, k, j)),
                  pl.BlockSpec((tm, tn), lambda i, j, k: (i, j)),
                  _gate_spec(gate3, tm, tn, rows_per_group)],
        out_specs=pl.BlockSpec((tm, tn), lambda i, j, k: (i, j)),
        out_shape=jax.ShapeDtypeStruct((m, n), F32),
        compiler_params=_cp("arbitrary", "arbitrary", "arbitrary"),
        name="mm_down",
    )(act, w, res, gate3)


UP_TM = 2048
UP_ROW_SPLIT = 4


def _up_prompt_kernel(x_ref, wg_ref, wv_ref, cw_ref, act_ref, tail_ref, carry_ref, wb_ref,
                      *, tiles_per_seq):
    i = pl.program_id(1)
    tm = x_ref.shape[0]

    @pl.when(i == 0)
    def _():
        tn = wg_ref.shape[1]
        wb_ref[:, 0:tn] = wg_ref[...].astype(BF16)
        wb_ref[:, tn:2 * tn] = wv_ref[...].astype(BF16)

    @pl.when(i % tiles_per_seq == 0)
    def _():
        carry_ref[...] = jnp.zeros_like(carry_ref)

    prev = carry_ref[...]
    cw = cw_ref[...]
    row8 = _iota(prev.shape, 0)
    th = tm // UP_ROW_SPLIT
    for r in range(UP_ROW_SPLIT):
        x = x_ref[r * th:(r + 1) * th, :]
        gv = _dg(x, wb_ref[...])
        g = gv[:, 0:gv.shape[1] // 2]
        v = gv[:, gv.shape[1] // 2:]
        hg = g * cw[2:3, :] + _shift_rows(g, 1) * cw[1:2, :] + _shift_rows(g, 2) * cw[0:1, :]
        top = (jnp.where(row8 < 1, pltpu.roll(prev, 1, axis=0), 0.0) * cw[1:2, :]
               + jnp.where(row8 < 2, pltpu.roll(prev, 2, axis=0), 0.0) * cw[0:1, :])
        hg = jnp.concatenate([hg[0:8, :] + top, hg[8:, :]], axis=0)
        act_ref[r * th:(r + 1) * th, :] = (_silu(hg) * v).astype(act_ref.dtype)
        prev = g[th - 8:th, :]
    carry_ref[...] = prev
    tail_ref[0] = prev


def _up_prompt(h2, w_up, layer, conv_w, tm, tn, seq):
    m, k = h2.shape
    nf = w_up.shape[2] // 2
    nj = nf // tn
    kern = functools.partial(_up_prompt_kernel, tiles_per_seq=seq // tm)
    return pl.pallas_call(
        kern,
        grid=(nj, m // tm),
        in_specs=[pl.BlockSpec((tm, k), lambda j, i: (i, 0)),
                  pl.BlockSpec((None, k, tn), lambda j, i: (layer, 0, j)),
                  pl.BlockSpec((None, k, tn), lambda j, i: (layer, 0, nj + j)),
                  pl.BlockSpec((3, tn), lambda j, i: (0, j))],
        out_specs=[pl.BlockSpec((tm, tn), lambda j, i: (i, j)),
                   pl.BlockSpec((1, 8, tn), lambda j, i: ((i * tm) // seq, 0, j))],
        out_shape=[jax.ShapeDtypeStruct((m, nf), BF16),
                   jax.ShapeDtypeStruct((m // seq, 8, nf), F32)],
        scratch_shapes=[pltpu.VMEM((8, tn), F32), pltpu.VMEM((k, 2 * tn), BF16)],
        compiler_params=_cp("arbitrary", "arbitrary"),
        name="up_prompt",
    )(h2, w_up, w_up, conv_w)


def _up_sample_kernel(x_ref, wg_ref, wv_ref, cw_ref, b0_ref, b1_ref, act_ref, g_ref):
    x = x_ref[...]
    g = _dg(x, wg_ref[...].astype(BF16))
    v = _dg(x, wv_ref[...].astype(BF16))
    cw = cw_ref[...]
    hg = b0_ref[...] * cw[0:1, :] + b1_ref[...] * cw[1:2, :] + g * cw[2:3, :]
    act_ref[...] = (_silu(hg) * v).astype(act_ref.dtype)
    g_ref[...] = g


def _up_sample(h2, w_up, layer, conv_w, buf0, buf1, tn):
    m, k = h2.shape
    nf = w_up.shape[2] // 2
    nj = nf // tn
    row_spec = pl.BlockSpec((m, tn), lambda j: (0, j))
    return pl.pallas_call(
        _up_sample_kernel,
        grid=(nj,),
        in_specs=[pl.BlockSpec((m, k), lambda j: (0, 0)),
                  pl.BlockSpec((None, k, tn), lambda j: (layer, 0, j)),
                  pl.BlockSpec((None, k, tn), lambda j: (layer, 0, nj + j)),
                  pl.BlockSpec((3, tn), lambda j: (0, j)),
                  row_spec, row_spec],
        out_specs=[row_spec, row_spec],
        out_shape=[jax.ShapeDtypeStruct((m, nf), BF16), jax.ShapeDtypeStruct((m, nf), F32)],
        compiler_params=_cp("arbitrary"),
        name="up_sample",
    )(h2, w_up, w_up, conv_w, buf0, buf1)


def _head_scalars(ps, alog_row, dt_row, fb_row):
    lane = _iota(ps.shape, 1)
    beta = jax.nn.sigmoid(ps)
    g = -jnp.exp(alog_row) * jax.nn.softplus(ps + dt_row)
    logf = jax.nn.log_sigmoid(ps + fb_row)
    return jnp.where(lane < L_G, beta, jnp.where(lane < L_F, g, logf))


def _aux_kernel(ps_ref, alog_ref, dt_ref, fb_ref, a1_ref, a2_ref, ft_ref):
    t = ps_ref.shape[1]
    a1 = _head_scalars(ps_ref[0], alog_ref[...], dt_ref[...], fb_ref[...])
    a1_ref[0] = a1
    c = GDN_CHUNK
    tril = (_iota((c, c), 0) >= _iota((c, c), 1)).astype(BF16)
    keep = (_iota((1, HEAD_DIM), 1) >= L_F).astype(F32)
    carry = jnp.zeros((1, HEAD_DIM), F32)
    for ci in range(t // c):
        cs = _mm_exact_lhs(tril, a1[ci * c:(ci + 1) * c, :]) + carry * keep
        a2_ref[0, ci * c:(ci + 1) * c, :] = cs
        carry = cs[c - 1:c, :]
    a2t = a2_ref[0].T
    ft_ref[0] = a2t[L_F:L_F + N_HEADS, :]


def _aux(ps3, alog_row, dt_row, fb_row):
    b, t, n = ps3.shape
    row = pl.BlockSpec((1, n), lambda i: (0, 0))
    blk = pl.BlockSpec((1, t, n), lambda i: (i, 0, 0))
    return pl.pallas_call(
        _aux_kernel,
        grid=(b,),
        in_specs=[blk, row, row, row],
        out_specs=[blk, blk, pl.BlockSpec((1, N_HEADS, t), lambda i: (i, 0, 0))],
        out_shape=[jax.ShapeDtypeStruct((b, t, n), F32), jax.ShapeDtypeStruct((b, t, n), F32),
                   jax.ShapeDtypeStruct((b, N_HEADS, t), F32)],
        compiler_params=_cp("arbitrary"),
        name="aux",
    )(ps3, alog_row, dt_row, fb_row)


GDN_HG = 2


GDN_INV_BLOCK = 16
GDN_PREP_CHUNKS = 8
GDN_CONV_ROWS = 128


def _unit_lower_inverses(lows):
    n = lows[0].shape[0]
    bsz = GDN_INV_BLOCK
    ri = _iota((n, n), 0)
    ci = _iota((n, n), 1)
    eye = (ri == ci).astype(F32)
    nb = n // 8
    in_diag = ri // bsz == ci // bsz
    xbs = [[eye[8 * r:8 * r + 8, :] for r in range(nb)] for _ in lows]
    lbs = [[jnp.where(in_diag, low, 0.0)[8 * r:8 * r + 8, :] for r in range(nb)] for low in lows]
    for j in range(bsz - 1):
        for xb, lb in zip(xbs, lbs):
            for b in range(n // bsz):
                col = b * bsz + j
                rj = xb[col // 8][col % 8:col % 8 + 1, :]
                for r in range(col // 8, (b + 1) * bsz // 8):
                    xb[r] = xb[r] - lb[r][:, col:col + 1] * rj
    invs = [jnp.concatenate(xb, axis=0) for xb in xbs]
    s = bsz
    while s < n:
        pair = ri // (2 * s) == ci // (2 * s)
        same = ri // s == ci // s
        belows = [jnp.where(pair, jnp.where(same, 0.0, low), 0.0) for low in lows]
        left = [_mm3(inv, below) for inv, below in zip(invs, belows)]
        invs = [inv - _mm3(lt, inv) for inv, lt in zip(invs, left)]
        s *= 2
    return invs


def _gdn_prompt_kernel(q_ref, k_ref, v_ref, z_ref, wq_ref, wk_ref, wv_ref, a1_ref, a2_ref, nw_ref,
                       o_ref, s_ref, qs, ks, vs, qp, op, ap, bp, egl):
    t = q_ref.shape[0]
    c = GDN_CHUNK
    nc = t // c
    hbase = pl.program_id(1) * GDN_HG

    rb = min(GDN_CONV_ROWS, t)

    def conv_block(ref, w_ref_, sl, blk):
        if blk == 0:
            x = jnp.concatenate([jnp.zeros((8, HEAD_DIM), F32), ref[0:rb, sl]], axis=0)
        else:
            x = ref[blk * rb - 8:(blk + 1) * rb, sl]
        w = w_ref_[:, sl]
        width = w.shape[0]
        y = x * w[width - 1:width, :]
        for s in range(1, width):
            y = y + pltpu.roll(x, s, axis=0) * w[width - 1 - s:width - s, :]
        return _silu(y[8:, :])

    for h in range(GDN_HG):
        sl = slice(h * HEAD_DIM, (h + 1) * HEAD_DIM)
        for blk in range(t // rb):
            rows = slice(blk * rb, (blk + 1) * rb)
            q = conv_block(q_ref, wq_ref, sl, blk)
            k = conv_block(k_ref, wk_ref, sl, blk)
            qs[h, rows, :] = q * lax.rsqrt(jnp.sum(q * q, axis=-1, keepdims=True) + EPS) * (HEAD_DIM ** -0.5)
            ks[h, rows, :] = k * lax.rsqrt(jnp.sum(k * k, axis=-1, keepdims=True) + EPS)
            vs[h, rows, :] = conv_block(v_ref, wv_ref, sl, blk)

    ri = _iota((c, c), 0)
    ci_ = _iota((c, c), 1)
    causal = ri >= ci_
    strict = ri > ci_
    eye = ri == ci_

    def prep(it, carry):
        chains = [(it * GDN_PREP_CHUNKS + cc, h) for cc in range(GDN_PREP_CHUNKS) for h in range(GDN_HG)]
        loaded = []
        for ci, h in chains:
            rows = pl.ds(pl.multiple_of(ci * c, c), c)
            loaded.append((a1_ref[0, rows, :], a2_ref[0, rows, :], qs[h, rows, :], ks[h, rows, :], vs[h, rows, :]))
        nch = range(len(chains))
        beta = [_lane_pick(loaded[i][0], L_BETA + hbase + chains[i][1]) for i in nch]
        gc = [_lane_pick(loaded[i][1], L_G + hbase + chains[i][1]) for i in nch]
        qv = [loaded[i][2] for i in nch]
        kv = [loaded[i][3] for i in nch]
        vv = [loaded[i][4] for i in nch]
        decay = []
        for i in nch:
            gc_row = jnp.sum(jnp.where(eye, gc[i], 0.0), axis=0, keepdims=True)
            decay.append(jnp.where(causal, jnp.exp(jnp.where(causal, gc[i] - gc_row, 0.0)), 0.0))
        kb = [kv[i] * beta[i] for i in nch]
        lows = [_mm1(kb[i], kv[i], NT) * jnp.where(strict, decay[i], 0.0) for i in nch]
        qk = [_mm1(qv[i], kv[i], NT) * decay[i] for i in nch]
        invs = _unit_lower_inverses(lows)
        eg = [jnp.exp(gc[i]) for i in nch]
        wu = [_mm3(invs[i], jnp.concatenate([kb[i] * eg[i], vv[i] * beta[i]], axis=1)) for i in nch]
        g_last = [gc[i][c - 1:c, :] for i in nch]
        kd = [kv[i] * jnp.exp(g_last[i] - gc[i]) for i in nch]
        kwu = [_mm1(kd[i], wu[i], TN) for i in nch]
        qwu = [_mm1(qk[i], wu[i]) for i in nch]
        results = [((qv[i] * eg[i] - qwu[i][:, :HEAD_DIM]).astype(BF16), qwu[i][:, HEAD_DIM:],
                    (-kwu[i][:, :HEAD_DIM]).astype(BF16), kwu[i][:, HEAD_DIM:],
                    jnp.broadcast_to(jnp.exp(g_last[i]), (8, HEAD_DIM))) for i in nch]
        for (ci, h), (qp_c, op_c, ap_c, bp_c, e_c) in zip(chains, results):
            rows = pl.ds(pl.multiple_of(ci * c, c), c)
            srows = pl.ds(pl.multiple_of(ci * HEAD_DIM, HEAD_DIM), HEAD_DIM)
            qp[h, rows, :] = qp_c
            op[h, rows, :] = op_c
            ap[h, srows, :] = ap_c
            bp[h, srows, :] = bp_c
            egl[h, pl.ds(pl.multiple_of(ci * 8, 8), 8), :] = e_c
        return carry

    lax.fori_loop(0, nc // GDN_PREP_CHUNKS, prep, 0)

    nw = nw_ref[...]

    def sweep(ci, states):
        rows = pl.ds(pl.multiple_of(ci * c, c), c)
        srows = pl.ds(pl.multiple_of(ci * HEAD_DIM, HEAD_DIM), HEAD_DIM)
        new_states = []
        for h in range(GDN_HG):
            s = states[h]
            sb = s.astype(BF16)
            o = _dg(qp[h, rows, :], sb) + op[h, rows, :]
            e_last = egl[h, pl.ds(pl.multiple_of(ci * 8, 8), 8), :][0:1, :]
            new_states.append(s * e_last + (_dg(ap[h, srows, :], sb) + bp[h, srows, :]))
            o = o * lax.rsqrt(jnp.mean(o * o, axis=-1, keepdims=True) + EPS) * nw
            z = z_ref[rows, h * HEAD_DIM:(h + 1) * HEAD_DIM]
            o_ref[rows, h * HEAD_DIM:(h + 1) * HEAD_DIM] = (o * _silu(z)).astype(o_ref.dtype)
        return tuple(new_states)

    init = tuple(jnp.zeros((HEAD_DIM, HEAD_DIM), F32) for _ in range(GDN_HG))
    final = lax.fori_loop(0, nc, sweep, init)
    for h in range(GDN_HG):
        s_ref[0, h] = final[h]


def _gdn_prompt(proj, a1, a2, conv_w, norm_w, b, t):
    hw = GDN_HG * HEAD_DIM
    ng = W_GROUP // hw
    col = lambda off: pl.BlockSpec((t, hw), lambda i, j: (i, off * ng + j))
    wcol = lambda off: pl.BlockSpec((4, hw), lambda i, j: (0, off * ng + j))
    aux = pl.BlockSpec((1, t, HEAD_DIM), lambda i, j: (i, 0, 0))
    nc = t // GDN_CHUNK
    scr = lambda rows, dt: pltpu.VMEM((GDN_HG, rows, HEAD_DIM), dt)
    return pl.pallas_call(
        _gdn_prompt_kernel,
        grid=(b, ng),
        in_specs=[col(0), col(1), col(2), col(3), wcol(0), wcol(1), wcol(2), aux, aux,
                  pl.BlockSpec((1, HEAD_DIM), lambda i, j: (0, 0))],
        out_specs=[pl.BlockSpec((t, hw), lambda i, j: (i, j)),
                   pl.BlockSpec((1, GDN_HG, HEAD_DIM, HEAD_DIM), lambda i, j: (i, j, 0, 0))],
        out_shape=[jax.ShapeDtypeStruct((b * t, W_GROUP), BF16),
                   jax.ShapeDtypeStruct((b, N_HEADS, HEAD_DIM, HEAD_DIM), F32)],
        scratch_shapes=[scr(t, F32), scr(t, F32), scr(t, F32), scr(t, BF16), scr(t, F32),
                        scr(nc * HEAD_DIM, BF16), scr(nc * HEAD_DIM, F32), scr(nc * 8, F32)],
        compiler_params=_cp("arbitrary", "arbitrary"),
        name="gdn_prompt",
    )(proj, proj, proj, proj, conv_w, conv_w, conv_w, a1, a2, norm_w.reshape(1, HEAD_DIM))


def _sconv_prompt_kernel(x_ref, gb_ref, gc_ref, w_ref, o_ref, tail_ref):
    u = gc_ref[...] * x_ref[...]
    t = u.shape[0]
    o_ref[...] = (gb_ref[...] * _causal_conv(u, w_ref[...])).astype(o_ref.dtype)
    tail_ref[0] = u[t - 8:t, :]


def _sconv_prompt(proj, w, b, t):
    tc = 256
    nb = W_GROUP // tc
    base = M_B // tc
    col = lambda k: pl.BlockSpec((t, tc), lambda i, j: (i, base + k * nb + j))
    return pl.pallas_call(
        _sconv_prompt_kernel,
        grid=(b, nb),
        in_specs=[col(0), col(1), col(2), pl.BlockSpec((3, tc), lambda i, j: (0, j))],
        out_specs=[pl.BlockSpec((t, tc), lambda i, j: (i, j)),
                   pl.BlockSpec((1, 8, tc), lambda i, j: (i, 0, j))],
        out_shape=[jax.ShapeDtypeStruct((b * t, W_GROUP), BF16),
                   jax.ShapeDtypeStruct((b, 8, W_GROUP), F32)],
        compiler_params=_cp("arbitrary", "arbitrary"),
        name="sconv_prompt",
    )(proj, proj, proj, w)


FOX_TQ = 512


def _fox_prompt_kernel(q_ref, k_ref, v_ref, ft_ref, o_ref):
    h = pl.program_id(1)
    qi = pl.program_id(2)
    q = (q_ref[...] * (HEAD_DIM ** -0.5)).astype(BF16)
    tq = q.shape[0]
    tk = tq
    hsel = _iota((N_HEADS, tk), 0) == h

    def tile(j, carry, diagonal):
        m, l, acc = carry
        start = pl.multiple_of(j * tk, tk)
        k = k_ref[pl.ds(start, tk), :].astype(BF16)
        v = v_ref[pl.ds(start, tk), :].astype(BF16)
        fk = jnp.sum(jnp.where(hsel, ft_ref[0, :, pl.ds(start, tk)], 0.0), axis=0, keepdims=True)
        s = _dg(q, k, NT) - fk
        if diagonal:
            s = jnp.where(_iota((tq, tk), 1) <= _iota((tq, tk), 0), s, -jnp.inf)
        m_new = jnp.maximum(m, jnp.max(s, axis=-1, keepdims=True))
        alpha = jnp.exp(m - m_new)
        p = jnp.exp(s - m_new)
        l = alpha * l + jnp.sum(p, axis=-1, keepdims=True)
        acc = alpha * acc + _dg(p.astype(BF16), v)
        return m_new, l, acc

    init = (jnp.full((tq, 1), -jnp.inf, F32), jnp.zeros((tq, 1), F32), jnp.zeros((tq, HEAD_DIM), F32))
    carry = lax.fori_loop(0, qi, lambda j, c: tile(j, c, False), init)
    _, l, acc = tile(qi, carry, True)
    o_ref[...] = (acc / l).astype(o_ref.dtype)


def _fox_prompt(proj, ft, b, t):
    tq = min(FOX_TQ, t)
    nq = t // tq
    base = M_C // HEAD_DIM
    return pl.pallas_call(
        _fox_prompt_kernel,
        grid=(b, N_HEADS, nq),
        in_specs=[pl.BlockSpec((tq, HEAD_DIM), lambda i, h, q: (i * nq + q, base + h)),
                  pl.BlockSpec((t, HEAD_DIM), lambda i, h, q: (i, base + N_HEADS + h)),
                  pl.BlockSpec((t, HEAD_DIM), lambda i, h, q: (i, base + 2 * N_HEADS + h)),
                  pl.BlockSpec((1, N_HEADS, t), lambda i, h, q: (i, 0, 0))],
        out_specs=pl.BlockSpec((tq, HEAD_DIM), lambda i, h, q: (i * nq + q, h)),
        out_shape=jax.ShapeDtypeStruct((b * t, W_GROUP), BF16),
        compiler_params=_cp("arbitrary", "arbitrary", "arbitrary"),
        name="fox_prompt",
    )(proj, proj, proj, ft)


def _lru_gates(xd, wr, br, wi, bi, lam):
    r = jax.nn.sigmoid(_mm3(xd, wr) + br)
    i = jax.nn.sigmoid(_mm3(xd, wi) + bi)
    log_a = -LRU_C * r * jax.nn.softplus(-lam)
    a = jnp.exp(log_a)
    th = jnp.tanh(log_a)
    one_minus_a2 = -2.0 * th / (1.0 - th)
    return a, jnp.sqrt(one_minus_a2) * (i * xd)


def _lru_prompt_kernel(x_ref, y_ref, cw_ref, cb_ref, wr_ref, br_ref, wi_ref, bi_ref, lam_ref,
                       o_ref, tail_ref):
    t = x_ref.shape[0]
    xd = _causal_conv(x_ref[...], cw_ref[...]) + cb_ref[...]
    a, bb = _lru_gates(xd, wr_ref[0], br_ref[...], wi_ref[0], bi_ref[...], lam_ref[...])
    row = _iota(a.shape, 0)
    s = 1
    while s < t:
        a_s = jnp.where(row < s, 1.0, pltpu.roll(a, s, axis=0))
        b_s = jnp.where(row < s, 0.0, pltpu.roll(bb, s, axis=0))
        bb = a * b_s + bb
        a = a * a_s
        s *= 2
    o_ref[...] = (jax.nn.gelu(y_ref[...]) * bb).astype(o_ref.dtype)
    tail_ref[0] = bb[t - 8:t, :]


def _lru_prompt(proj, cw, cb, wr, br, wi, bi, lam, b, t):
    n = HEAD_DIM
    nb = W_GROUP // n
    base = M_D // n
    vec = lambda: pl.BlockSpec((1, n), lambda i, j: (0, j))
    mat = lambda: pl.BlockSpec((1, n, n), lambda i, j: (j, 0, 0))
    r1 = lambda a: a.reshape(1, W_GROUP)
    return pl.pallas_call(
        _lru_prompt_kernel,
        grid=(b, nb),
        in_specs=[pl.BlockSpec((t, n), lambda i, j: (i, base + j)),
                  pl.BlockSpec((t, n), lambda i, j: (i, base + nb + j)),
                  pl.BlockSpec((4, n), lambda i, j: (0, j)),
                  vec(), mat(), vec(), mat(), vec(), vec()],
        out_specs=[pl.BlockSpec((t, n), lambda i, j: (i, j)),
                   pl.BlockSpec((1, 8, n), lambda i, j: (i, 0, j))],
        out_shape=[jax.ShapeDtypeStruct((b * t, W_GROUP), BF16),
                   jax.ShapeDtypeStruct((b, 8, W_GROUP), F32)],
        compiler_params=_cp("arbitrary", "arbitrary"),
        name="lru_prompt",
    )(proj, proj, cw, r1(cb), wr, r1(br), wi, r1(bi), r1(lam))


def _gdn_sample_kernel(q_ref, k_ref, v_ref, z_ref, b0_ref, b1_ref, b2_ref, cw_ref, ps_ref,
                       alog_ref, dt_ref, fb_ref, nw_ref, s_ref, o_ref, so_ref, hs_ref):
    i = pl.program_id(0)
    cw = cw_ref[...]
    hs = _head_scalars(ps_ref[...], alog_ref[...], dt_ref[...], fb_ref[...])
    hs_ref[...] = hs
    hs_i = jnp.sum(jnp.where(_iota(hs.shape, 0) == i, hs, 0.0), axis=0, keepdims=True)
    row_i = lambda ref: ref[pl.ds(i, 1), :]
    b0 = row_i(b0_ref)
    b1 = row_i(b1_ref)
    b2 = row_i(b2_ref)
    eye = _iota((HEAD_DIM, HEAD_DIM), 0) == _iota((HEAD_DIM, HEAD_DIM), 1)
    nw = nw_ref[...]

    def conv(x_row, off):
        w = cw[:, off:off + W_GROUP]
        bo = slice(off, off + W_GROUP)
        return _silu(b0[:, bo] * w[0:1, :] + b1[:, bo] * w[1:2, :] + b2[:, bo] * w[2:3, :] + x_row * w[3:4, :])

    q_all = conv(row_i(q_ref), 0)
    k_all = conv(row_i(k_ref), W_GROUP)
    v_all = conv(row_i(v_ref), 2 * W_GROUP)
    z_all = row_i(z_ref)
    for h in range(N_HEADS):
        sl = slice(h * HEAD_DIM, (h + 1) * HEAD_DIM)
        q = q_all[:, sl]
        k = k_all[:, sl]
        v = v_all[:, sl]
        q = q * lax.rsqrt(jnp.sum(q * q, axis=-1, keepdims=True) + EPS) * (HEAD_DIM ** -0.5)
        k = k * lax.rsqrt(jnp.sum(k * k, axis=-1, keepdims=True) + EPS)
        beta = hs_i[:, L_BETA + h:L_BETA + h + 1]
        g = hs_i[:, L_G + h:L_G + h + 1]
        k_col = jnp.sum(jnp.where(eye, k, 0.0), axis=1, keepdims=True)
        q_col = jnp.sum(jnp.where(eye, q, 0.0), axis=1, keepdims=True)
        s = s_ref[0, h] * jnp.exp(g)
        delta = (v - jnp.sum(k_col * s, axis=0, keepdims=True)) * beta
        s = s + k_col * delta
        so_ref[0, h] = s
        o = jnp.sum(q_col * s, axis=0, keepdims=True)
        o = o * lax.rsqrt(jnp.mean(o * o, axis=-1, keepdims=True) + EPS) * nw
        o_ref[0, :, sl] = o * _silu(z_all[:, sl])


def _gdn_sample(proj, ps, buf, conv_w, alog_row, dt_row, fb_row, norm_w, state, bs):
    rows = proj.shape[0]
    col = lambda k: pl.BlockSpec((rows, W_GROUP), lambda i: (0, k))
    full = lambda a: pl.BlockSpec(a.shape, lambda i: (0,) * a.ndim)
    prow = pl.BlockSpec((1, HEAD_DIM), lambda i: (0, 0))
    sblk = pl.BlockSpec((1, N_HEADS, HEAD_DIM, HEAD_DIM), lambda i: (i, 0, 0, 0))
    return pl.pallas_call(
        _gdn_sample_kernel,
        grid=(bs,),
        in_specs=[col(0), col(1), col(2), col(3), full(buf[0]), full(buf[1]), full(buf[2]),
                  full(conv_w), pl.BlockSpec((rows, HEAD_DIM), lambda i: (0, 0)), prow, prow, prow, prow, sblk],
        out_specs=[pl.BlockSpec((1, 1, W_GROUP), lambda i: (i, 0, 0)), sblk,
                   pl.BlockSpec((rows, HEAD_DIM), lambda i: (0, 0))],
        out_shape=[jax.ShapeDtypeStruct((bs, 1, W_GROUP), F32),
                   jax.ShapeDtypeStruct(state.shape, F32),
                   jax.ShapeDtypeStruct((rows, HEAD_DIM), F32)],
        compiler_params=_cp("arbitrary"),
        name="gdn_sample",
    )(proj, proj, proj, proj, buf[0], buf[1], buf[2], conv_w, ps, alog_row, dt_row, fb_row,
      norm_w.reshape(1, HEAD_DIM), state)


def _bd_sample_kernel(xb_ref, gb_ref, gc_ref, sb0_ref, sb1_ref, sw_ref,
                      xd_ref, yd_ref, lb0_ref, lb1_ref, lb2_ref, lh_ref, cw_ref, cb_ref,
                      wr_ref, br_ref, wi_ref, bi_ref, lam_ref,
                      ob_ref, u_ref, od_ref, h_ref):
    sw = sw_ref[...]
    u = gc_ref[...] * xb_ref[...]
    ob_ref[...] = gb_ref[...] * (sb0_ref[...] * sw[0:1, :] + sb1_ref[...] * sw[1:2, :] + u * sw[2:3, :])
    u_ref[...] = u
    cw = cw_ref[...]
    xd = (lb0_ref[...] * cw[0:1, :] + lb1_ref[...] * cw[1:2, :] + lb2_ref[...] * cw[2:3, :]
          + xd_ref[...] * cw[3:4, :] + cb_ref[...])
    n = HEAD_DIM
    for j in range(W_GROUP // n):
        sl = slice(j * n, (j + 1) * n)
        a, bb = _lru_gates(xd[:, sl], wr_ref[j], br_ref[:, sl], wi_ref[j], bi_ref[:, sl], lam_ref[:, sl])
        hn = a * lh_ref[:, sl] + bb
        h_ref[:, sl] = hn
        od_ref[:, sl] = jax.nn.gelu(yd_ref[:, sl]) * hn


def _bd_sample(proj, sbuf, sw, lbuf, lh, cw, cb, wr, br, wi, bi, lam):
    rows = proj.shape[0]
    r1 = lambda a: a.reshape(1, W_GROUP)
    pcol = lambda c: pl.BlockSpec((rows, W_GROUP), lambda i: (0, c // W_GROUP))
    full = lambda a: pl.BlockSpec(a.shape, lambda i: (0,) * a.ndim)
    ins = [proj, proj, proj, sbuf[0], sbuf[1], sw, proj, proj, lbuf[0], lbuf[1], lbuf[2], lh, cw,
           r1(cb), wr, r1(br), wi, r1(bi), r1(lam)]
    specs = [pcol(M_B), pcol(M_B + W_GROUP), pcol(M_B + 2 * W_GROUP)] + [full(a) for a in ins[3:6]] \
        + [pcol(M_D), pcol(M_D + W_GROUP)] + [full(a) for a in ins[8:]]
    out = jax.ShapeDtypeStruct((rows, W_GROUP), F32)
    ospec = pl.BlockSpec((rows, W_GROUP), lambda i: (0, 0))
    return pl.pallas_call(
        _bd_sample_kernel,
        grid=(1,),
        in_specs=specs,
        out_specs=[ospec] * 4,
        out_shape=[out] * 4,
        compiler_params=_cp("arbitrary"),
        name="bd_sample",
    )(*ins)


FOX_PAGES_PER_STEP = 8


def _fox_sample_kernel(pt_ref, q_ref, kn_ref, vn_ref, r0_ref, *refs):
    del pt_ref
    n = FOX_PAGES_PER_STEP
    kp_refs, vp_refs, lf_refs = refs[0:n], refs[n:2 * n], refs[2 * n:3 * n]
    o_ref, m_sc, l_sc, acc_sc, r_sc = refs[3 * n:]
    i = pl.program_id(0)
    p = pl.program_id(1)
    npg = pl.num_programs(1)
    scale = HEAD_DIM ** -0.5
    q = q_ref[i]

    @pl.when(p == 0)
    def _():
        m_sc[...] = jnp.sum(q * kn_ref[i], axis=-1, keepdims=True) * scale
        l_sc[...] = jnp.ones_like(l_sc)
        acc_sc[...] = vn_ref[i]
        r_sc[...] = r0_ref[pl.ds(i, 1), :]

    m = m_sc[...]
    l = l_sc[...]
    acc = acc_sc[...]
    r = r_sc[...]
    pg, nh, hd = kp_refs[0].shape
    flat = pg * nh
    qb = q.astype(BF16)
    lane = _iota((1, flat), 1)
    own = _iota((nh, flat), 1) % nh == _iota((nh, flat), 0)
    pages = range(n)
    lfs = [lf_refs[t][...] for t in pages]
    suffix = list(lfs)
    total = list(lfs)
    step = nh
    while step < flat:
        suffix = [x + jnp.where(lane + step < flat, pltpu.roll(x, flat - step, axis=1), 0.0) for x in suffix]
        total = [x + pltpu.roll(x, step, axis=1) for x in total]
        step *= 2
    scores = [_dg(qb, kp_refs[t][...].reshape(flat, hd).astype(BF16), NT) * scale for t in pages]
    s = []
    for t in pages:
        s.append(jnp.where(own, scores[t] + ((suffix[t] - lfs[t]) + r), -jnp.inf))
        r = r + total[t]
    m_new = m
    for t in pages:
        m_new = jnp.maximum(m_new, jnp.max(s[t], axis=-1, keepdims=True))
    alpha = jnp.exp(m - m_new)
    pw = [jnp.exp(s[t] - m_new) for t in pages]
    l = alpha * l
    acc = alpha * acc
    for t in pages:
        l = l + jnp.sum(pw[t], axis=-1, keepdims=True)
        acc = acc + _dg(pw[t].astype(BF16), vp_refs[t][...].reshape(flat, hd).astype(BF16))
    m_sc[...] = m_new
    l_sc[...] = l
    acc_sc[...] = acc
    r_sc[...] = r

    @pl.when(p == npg - 1)
    def _():
        o_ref[0] = acc / l


def _fox_sample(q3, kn3, vn3, logf_new, page_table, ck, cv, clf, layer, bs):
    rows = q3.shape[0]
    n_pages = page_table.shape[1]
    page = ck.shape[2]
    flat = page * N_HEADS
    n = FOX_PAGES_PER_STEP
    assert n_pages % n == 0
    r0 = jnp.tile(logf_new, (1, page))
    clf_flat = clf.reshape(clf.shape[0], clf.shape[1], 1, flat)
    new_spec = pl.BlockSpec((rows, N_HEADS, HEAD_DIM), lambda i, p, pt: (0, 0, 0))

    def kv_spec(t):
        return pl.BlockSpec((None, None, page, N_HEADS, HEAD_DIM),
                            lambda i, p, pt: (layer, pt[i, n_pages - 1 - (p * n + t)], 0, 0, 0))

    def lf_spec(t):
        return pl.BlockSpec((None, None, 1, flat),
                            lambda i, p, pt: (layer, pt[i, n_pages - 1 - (p * n + t)], 0, 0))

    slots = list(range(n))
    grid_spec = pltpu.PrefetchScalarGridSpec(
        num_scalar_prefetch=1,
        grid=(bs, n_pages // n),
        in_specs=[new_spec, new_spec, new_spec, pl.BlockSpec((rows, flat), lambda i, p, pt: (0, 0))]
        + [kv_spec(t) for t in slots] + [kv_spec(t) for t in slots] + [lf_spec(t) for t in slots],
        out_specs=pl.BlockSpec((1, N_HEADS, HEAD_DIM), lambda i, p, pt: (i, 0, 0)),
        scratch_shapes=[pltpu.VMEM((N_HEADS, 1), F32), pltpu.VMEM((N_HEADS, 1), F32),
                        pltpu.VMEM((N_HEADS, HEAD_DIM), F32), pltpu.VMEM((1, flat), F32)],
    )
    return pl.pallas_call(
        _fox_sample_kernel,
        grid_spec=grid_spec,
        out_shape=jax.ShapeDtypeStruct((bs, N_HEADS, HEAD_DIM), F32),
        compiler_params=_cp("arbitrary", "arbitrary"),
        name="fox_sample",
    )(page_table, q3, kn3, vn3, r0, *([ck] * n), *([cv] * n), *([clf_flat] * n))


def _prep_weights(w_in, w_out, w_up, w_down):
    w_main = jnp.concatenate([w_in[:, :, 0:OFF_A_BETA], w_in[:, :, OFF_B:OFF_C_F], w_in[:, :, OFF_D:N_IN]],
                             axis=2).astype(BF16)
    w_small = jnp.concatenate([w_in[:, :, OFF_A_BETA:OFF_B], w_in[:, :, OFF_C_F:OFF_D],
                               jnp.zeros(w_in.shape[:2] + (HEAD_DIM - 3 * N_HEADS,), w_in.dtype)],
                              axis=2).astype(BF16)
    return dict(w_main=w_main, w_small=w_small, w_out=w_out.astype(BF16), w_up=w_up,
                w_down=w_down.astype(BF16))


def _lane_row(v, off):
    return jnp.zeros((1, HEAD_DIM), F32).at[0, off:off + N_HEADS].set(v)


def _pad_rows(a, rows):
    return jnp.pad(a, ((0, rows - a.shape[0]),) + ((0, 0),) * (a.ndim - 1))


def kernel(x_prompt, x_sample, cache_k, cache_v, cache_logf, state_gdn, state_gdn_conv, state_sconv,
           state_lru_h, state_lru_conv, state_ffn_conv, page_table, c_prompt, c_sample, ada_w, ada_b,
           ln1_w, ln2_w, w_in, gdn_conv_w, gdn_a_log, gdn_dt_bias, gdn_norm_w, sconv_w, fox_f_bias,
           lru_conv_w, lru_conv_b, lru_wr, lru_br, lru_wi, lru_bi, lru_lambda, w_out, ffn_conv_w,
           w_up, w_down, final_norm_w):
    bp, t, d = x_prompt.shape
    bs = x_sample.shape[0]
    depth = w_in.shape[0]
    sr = SAMPLE_ROWS
    tm = min(1024, t)
    tnorm = min(512, t)
    c16 = _pad_rows(jnp.concatenate([c_prompt, c_sample], axis=0), sr)

    xp = x_prompt.reshape(bp * t, d)
    xs = _pad_rows(x_sample.reshape(bs, d), sr)
    p_states, s_states = [], []
    w = _prep_weights(w_in, w_out, w_up, w_down)
    for l in range(depth):
        alog_row = _lane_row(gdn_a_log[l], L_G)
        dt_row = _lane_row(gdn_dt_bias[l], L_G)
        fb_row = _lane_row(fox_f_bias[l], L_F)
        ffn_cw = ffn_conv_w[l]
        mod = _ada(c16, ada_w, ada_b, l).reshape(sr, 6, d)
        pm = [mod[0:bp, k][:, None, :] for k in range(6)]
        sm = [_pad_rows(mod[bp:bp + bs, k], sr)[None] for k in range(6)]

        h = _norm_mod(xp.reshape(bp, t, d), ln1_w[l], pm[1], pm[0], tnorm).reshape(bp * t, d)
        proj = _matmul(h, w["w_main"], l, tm, 1024, "mm_in")
        ps = _matmul(h, w["w_small"], l, tm, HEAD_DIM, "mm_in_small")
        a1, a2, ft = _aux(ps.reshape(bp, t, HEAD_DIM), alog_row, dt_row, fb_row)
        oa, p_gdn = _gdn_prompt(proj, a1, a2, gdn_conv_w[l], gdn_norm_w[l], bp, t)
        ob, sconv_tail = _sconv_prompt(proj, sconv_w[l], bp, t)
        oc = _fox_prompt(proj, ft, bp, t)
        od, lru_tail = _lru_prompt(proj, lru_conv_w[l], lru_conv_b[l], lru_wr[l], lru_br[l], lru_wi[l],
                                   lru_bi[l], lru_lambda[l], bp, t)
        xp = _mm_out([oa, ob, oc, od], w["w_out"], l, xp, pm[2], tm, 1024, t)
        h2 = _norm_mod(xp.reshape(bp, t, d), ln2_w[l], pm[4], pm[3], tnorm).reshape(bp * t, d)
        act, ffn_tail = _up_prompt(h2, w["w_up"], l, ffn_cw, min(UP_TM, t), UP_TN, t)
        xp = _mm_down(act, w["w_down"], l, xp, pm[5], min(512, t), 512, D_FF, t)

        proj3 = proj.reshape(bp, t, N_MAIN)
        p_states.append((
            proj3[:, :, M_C + W_GROUP:M_C + 2 * W_GROUP].reshape(bp, t, N_HEADS, HEAD_DIM),
            proj3[:, :, M_C + 2 * W_GROUP:M_C + 3 * W_GROUP].reshape(bp, t, N_HEADS, HEAD_DIM),
            a1[:, :, L_F:L_F + N_HEADS],
            p_gdn,
            proj3[:, t - 3:, M_A:M_A + 3 * W_GROUP],
            sconv_tail[:, 6:8, :],
            lru_tail[:, 7, :],
            proj3[:, t - 3:, M_D:M_D + W_GROUP],
            ffn_tail[:, 6:8, :],
        ))

        hs_in = _norm_mod(xs[None], ln1_w[l], sm[1], sm[0], sr)[0]
        sproj = _matmul(hs_in, w["w_main"], l, sr, 2048, "mm_in_s")
        sps = _matmul(hs_in, w["w_small"], l, sr, HEAD_DIM, "mm_in_small_s")
        gbuf = [_pad_rows(state_gdn_conv[l][:, j, :], sr) for j in range(3)]
        soa, s_gdn, shs = _gdn_sample(sproj, sps, gbuf, gdn_conv_w[l], alog_row, dt_row,
                                      fb_row, gdn_norm_w[l], state_gdn[l], bs)
        sbuf = [_pad_rows(state_sconv[l][:, j, :], sr) for j in range(2)]
        lbuf = [_pad_rows(state_lru_conv[l][:, j, :], sr) for j in range(3)]
        sob, s_u, sod, s_h = _bd_sample(sproj, sbuf, sconv_w[l], lbuf, _pad_rows(state_lru_h[l], sr),
                                        lru_conv_w[l], lru_conv_b[l], lru_wr[l], lru_br[l], lru_wi[l],
                                        lru_bi[l], lru_lambda[l])
        heads3 = lambda off: sproj[:, off:off + W_GROUP].reshape(sr, N_HEADS, HEAD_DIM)
        soc = _fox_sample(heads3(M_C), heads3(M_C + W_GROUP), heads3(M_C + 2 * W_GROUP),
                          shs[:, L_F:L_F + N_HEADS], page_table, cache_k, cache_v, cache_logf, l, bs)
        parts = [_pad_rows(soa.reshape(bs, W_GROUP), sr).astype(BF16), sob.astype(BF16),
                 _pad_rows(soc.reshape(bs, W_GROUP), sr).astype(BF16), sod.astype(BF16)]
        xs = _mm_out(parts, w["w_out"], l, xs, sm[2], sr, 2048, sr)
        sh2 = _norm_mod(xs[None], ln2_w[l], sm[4], sm[3], sr)[0]
        fbuf = [_pad_rows(state_ffn_conv[l][:, j, :], sr) for j in range(2)]
        sact, s_g = _up_sample(sh2, w["w_up"], l, ffn_cw, fbuf[0], fbuf[1], UP_TN)
        xs = _mm_down(sact, w["w_down"], l, xs, sm[5], sr, 512, D_FF, sr)

        sp = sproj[:bs]
        s_states.append((
            sp[:, M_C + W_GROUP:M_C + 2 * W_GROUP].reshape(bs, 1, N_HEADS, HEAD_DIM),
            sp[:, M_C + 2 * W_GROUP:M_C + 3 * W_GROUP].reshape(bs, 1, N_HEADS, HEAD_DIM),
            shs[:bs, L_F:L_F + N_HEADS].reshape(bs, 1, N_HEADS),
            s_gdn,
            jnp.concatenate([state_gdn_conv[l][:, 1:, :], sp[:, None, M_A:M_A + 3 * W_GROUP]], axis=1),
            jnp.concatenate([state_sconv[l][:, 1:, :], s_u[:bs, None, :]], axis=1),
            s_h[:bs],
            jnp.concatenate([state_lru_conv[l][:, 1:, :], sp[:, None, M_D:M_D + W_GROUP]], axis=1),
            jnp.concatenate([state_ffn_conv[l][:, 1:, :], s_g[:bs, None, :]], axis=1),
        ))

    y_prompt = _rmsnorm(xp, final_norm_w, tnorm).reshape(bp, t, d)
    y_sample = _rmsnorm(xs, final_norm_w, sr)[:bs].reshape(bs, 1, d)
    p_out = tuple(jnp.stack(parts) for parts in zip(*p_states))
    s_out = tuple(jnp.stack(parts) for parts in zip(*s_states))
    return (y_prompt, y_sample) + p_out + s_out
```

```python
import functools

import jax
import jax.numpy as jnp
from jax import lax
from jax.experimental import pallas as pl
from jax.experimental.pallas import tpu as pltpu

F32 = jnp.float32
BF16 = jnp.bfloat16

D_MODEL = 4096
HEAD_DIM = 128
W_GROUP = D_MODEL // 4
N_HEADS = W_GROUP // HEAD_DIM
GDN_CHUNK = 64
D_FF = 11008
UP_TN = 256
LRU_C = 8.0
EPS = 1e-6
SAMPLE_ROWS = 16

OFF_A_BETA = 4 * W_GROUP
OFF_B = OFF_A_BETA + 2 * N_HEADS
OFF_C_F = OFF_B + 6 * W_GROUP
OFF_D = OFF_C_F + N_HEADS
N_IN = OFF_D + 2 * W_GROUP
M_A = 0
M_B = 4 * W_GROUP
M_C = 7 * W_GROUP
M_D = 10 * W_GROUP
N_MAIN = 12 * W_GROUP
L_BETA, L_G, L_F = 0, N_HEADS, 2 * N_HEADS

VMEM_LIMIT = 56 * 1024 * 1024

NN = ((1,), (0,))
NT = ((1,), (1,))
TN = ((0,), (0,))


def _cp(*sem):
    return pltpu.CompilerParams(dimension_semantics=sem, vmem_limit_bytes=VMEM_LIMIT)


def _dg(a, b, dims=NN):
    return lax.dot_general(a, b, (dims, ((), ())), preferred_element_type=F32)


def _mm1(a, b, dims=NN):
    return _dg(a.astype(BF16), b.astype(BF16), dims)


def _split(a):
    hi = a.astype(BF16)
    lo = (a - hi.astype(F32)).astype(BF16)
    return hi, lo


def _mm3(a, b, dims=NN):
    ah, al = _split(a)
    bh, bl = _split(b)
    return _dg(ah, bh, dims) + (_dg(ah, bl, dims) + _dg(al, bh, dims))


def _mm_exact_lhs(a_bf16, b):
    b0 = b.astype(BF16)
    r1 = b - b0.astype(F32)
    b1 = r1.astype(BF16)
    b2 = (r1 - b1.astype(F32)).astype(BF16)
    return _dg(a_bf16, b0) + (_dg(a_bf16, b1) + _dg(a_bf16, b2))


def _mm_exact_rhs(a, b_bf16):
    a0 = a.astype(BF16)
    r1 = a - a0.astype(F32)
    a1 = r1.astype(BF16)
    a2 = (r1 - a1.astype(F32)).astype(BF16)
    return _dg(a0, b_bf16) + (_dg(a1, b_bf16) + _dg(a2, b_bf16))


def _silu(x):
    return x * jax.nn.sigmoid(x)


def _iota(shape, axis):
    return lax.broadcasted_iota(jnp.int32, shape, axis)


def _shift_rows(x, s):
    r = pltpu.roll(x, s, axis=0)
    return jnp.where(_iota(x.shape, 0) < s, 0.0, r)


def _lane_pick(x, lane):
    return jnp.sum(jnp.where(_iota(x.shape, 1) == lane, x, 0.0), axis=-1, keepdims=True)


def _causal_conv(x, w):
    width = w.shape[0]
    y = x * w[width - 1:width, :]
    for s in range(1, width):
        y = y + _shift_rows(x, s) * w[width - 1 - s:width - s, :]
    return y


ADA_TN = 1024


def _ada_kernel(c_ref, w_ref, b_ref, o_ref):
    c = c_ref[...]
    o_ref[...] = _mm1(_silu(c), w_ref[...]) + b_ref[...]


def _ada(c16, w, b, layer):
    _, k, n = w.shape
    tn = ADA_TN
    return pl.pallas_call(
        _ada_kernel,
        grid=(n // tn,),
        in_specs=[pl.BlockSpec((SAMPLE_ROWS, k), lambda j: (0, 0)),
                  pl.BlockSpec((None, k, tn), lambda j: (layer, 0, j)),
                  pl.BlockSpec((None, 1, tn), lambda j: (layer, 0, j))],
        out_specs=pl.BlockSpec((SAMPLE_ROWS, tn), lambda j: (0, j)),
        out_shape=jax.ShapeDtypeStruct((SAMPLE_ROWS, n), F32),
        compiler_params=_cp("arbitrary"),
        name="ada",
    )(c16, w, b.reshape(b.shape[0], 1, n))


def _norm_mod_kernel(x_ref, w_ref, sc_ref, sh_ref, o_ref):
    x = x_ref[0]
    y = x * lax.rsqrt(jnp.mean(x * x, axis=-1, keepdims=True) + EPS) * w_ref[...]
    o_ref[0] = (y * (1.0 + sc_ref[0]) + sh_ref[0]).astype(o_ref.dtype)


def _norm_mod(x3, w, scale3, shift3, tt):
    g, r, d = x3.shape
    rm = scale3.shape[1]
    if rm == 1:
        mod_spec = pl.BlockSpec((1, 1, d), lambda b, t: (b, 0, 0))
    else:
        mod_spec = pl.BlockSpec((1, tt, d), lambda b, t: (b, t, 0))
    return pl.pallas_call(
        _norm_mod_kernel,
        grid=(g, r // tt),
        in_specs=[pl.BlockSpec((1, tt, d), lambda b, t: (b, t, 0)),
                  pl.BlockSpec((1, d), lambda b, t: (0, 0)),
                  mod_spec, mod_spec],
        out_specs=pl.BlockSpec((1, tt, d), lambda b, t: (b, t, 0)),
        out_shape=jax.ShapeDtypeStruct((g, r, d), BF16),
        compiler_params=_cp("arbitrary", "arbitrary"),
        name="norm_mod",
    )(x3, w.reshape(1, d), scale3, shift3)


def _rms_kernel(x_ref, w_ref, o_ref):
    x = x_ref[...]
    o_ref[...] = x * lax.rsqrt(jnp.mean(x * x, axis=-1, keepdims=True) + EPS) * w_ref[...]


def _rmsnorm(x2, w, tt):
    r, d = x2.shape
    return pl.pallas_call(
        _rms_kernel,
        grid=(r // tt,),
        in_specs=[pl.BlockSpec((tt, d), lambda t: (t, 0)), pl.BlockSpec((1, d), lambda t: (0, 0))],
        out_specs=pl.BlockSpec((tt, d), lambda t: (t, 0)),
        out_shape=jax.ShapeDtypeStruct((r, d), F32),
        compiler_params=_cp("arbitrary"),
        name="rmsnorm",
    )(x2, w.reshape(1, d))


def _mm_kernel(x_ref, w_ref, o_ref):
    o_ref[...] = _dg(x_ref[...], w_ref[...])


def _matmul(x, w, layer, tm, tn, name):
    m, k = x.shape
    n = w.shape[2]
    return pl.pallas_call(
        _mm_kernel,
        grid=(m // tm, n // tn),
        in_specs=[pl.BlockSpec((tm, k), lambda i, j: (i, 0)),
                  pl.BlockSpec((None, k, tn), lambda i, j: (layer, 0, j))],
        out_specs=pl.BlockSpec((tm, tn), lambda i, j: (i, j)),
        out_shape=jax.ShapeDtypeStruct((m, n), F32),
        compiler_params=_cp("arbitrary", "arbitrary"),
        name=name,
    )(x, w)


def _mm_out_kernel(a_ref, b_ref, c_ref, d_ref, w_ref, res_ref, gate_ref, o_ref):
    g = a_ref.shape[1]
    acc = _dg(a_ref[...], w_ref[0:g, :])
    acc = acc + _dg(b_ref[...], w_ref[g:2 * g, :])
    acc = acc + _dg(c_ref[...], w_ref[2 * g:3 * g, :])
    acc = acc + _dg(d_ref[...], w_ref[3 * g:4 * g, :])
    o_ref[...] = res_ref[...] + gate_ref[0] * acc


def _gate_spec(gate3, tm, tn, rows_per_group):
    if gate3.shape[1] == 1:
        return pl.BlockSpec((1, 1, tn), lambda i, j, *_: ((i * tm) // rows_per_group, 0, j))
    return pl.BlockSpec((1, tm, tn), lambda i, j, *_: (0, i, j))


def _mm_out(parts, w, layer, res, gate3, tm, tn, rows_per_group):
    m, g = parts[0].shape
    n = w.shape[2]
    part_spec = pl.BlockSpec((tm, g), lambda i, j: (i, 0))
    return pl.pallas_call(
        _mm_out_kernel,
        grid=(m // tm, n // tn),
        in_specs=[part_spec, part_spec, part_spec, part_spec,
                  pl.BlockSpec((None, 4 * g, tn), lambda i, j: (layer, 0, j)),
                  pl.BlockSpec((tm, tn), lambda i, j: (i, j)),
                  _gate_spec(gate3, tm, tn, rows_per_group)],
        out_specs=pl.BlockSpec((tm, tn), lambda i, j: (i, j)),
        out_shape=jax.ShapeDtypeStruct((m, n), F32),
        compiler_params=_cp("arbitrary", "arbitrary"),
        name="mm_out",
    )(*parts, w, res, gate3)


def _down_kernel(a_ref, w_ref, res_ref, gate_ref, o_ref):
    k = pl.program_id(2)
    part = gate_ref[0] * _dg(a_ref[...], w_ref[...])

    @pl.when(k == 0)
    def _():
        o_ref[...] = res_ref[...] + part

    @pl.when(k > 0)
    def _():
        o_ref[...] += part


def _mm_down(act, w, layer, res, gate3, tm, tn, tk, rows_per_group):
    m, kk = act.shape
    n = w.shape[2]
    return pl.pallas_call(
        _down_kernel,
        grid=(m // tm, n // tn, kk // tk),
        in_specs=[pl.BlockSpec((tm, tk), lambda i, j, k: (i, k)),
                  pl.BlockSpec((None, tk, tn), lambda i, j, k: (layer, k, j)),
                  pl.BlockSpec((tm, tn), lambda i, j, k: (i, j)),
                  _gate_spec(gate3, tm, tn, rows_per_group)],
        out_specs=pl.BlockSpec((tm, tn), lambda i, j, k: (i, j)),
        out_shape=jax.ShapeDtypeStruct((m, n), F32),
        compiler_params=_cp("arbitrary", "arbitrary", "arbitrary"),
        name="mm_down",
    )(act, w, res, gate3)


UP_TM = 2048
UP_ROW_SPLIT = 4


def _up_prompt_kernel(x_ref, wg_ref, wv_ref, cw_ref, act_ref, tail_ref, carry_ref, wb_ref,
                      *, tiles_per_seq):
    i = pl.program_id(1)
    tm = x_ref.shape[0]

    @pl.when(i == 0)
    def _():
        tn = wg_ref.shape[1]
        wb_ref[:, 0:tn] = wg_ref[...].astype(BF16)
        wb_ref[:, tn:2 * tn] = wv_ref[...].astype(BF16)

    @pl.when(i % tiles_per_seq == 0)
    def _():
        carry_ref[...] = jnp.zeros_like(carry_ref)

    prev = carry_ref[...]
    cw = cw_ref[...]
    row8 = _iota(prev.shape, 0)
    th = tm // UP_ROW_SPLIT
    for r in range(UP_ROW_SPLIT):
        x = x_ref[r * th:(r + 1) * th, :]
        gv = _dg(x, wb_ref[...])
        g = gv[:, 0:gv.shape[1] // 2]
        v = gv[:, gv.shape[1] // 2:]
        hg = g * cw[2:3, :] + _shift_rows(g, 1) * cw[1:2, :] + _shift_rows(g, 2) * cw[0:1, :]
        top = (jnp.where(row8 < 1, pltpu.roll(prev, 1, axis=0), 0.0) * cw[1:2, :]
               + jnp.where(row8 < 2, pltpu.roll(prev, 2, axis=0), 0.0) * cw[0:1, :])
        hg = jnp.concatenate([hg[0:8, :] + top, hg[8:, :]], axis=0)
        act_ref[r * th:(r + 1) * th, :] = (_silu(hg) * v).astype(act_ref.dtype)
        prev = g[th - 8:th, :]
    carry_ref[...] = prev
    tail_ref[0] = prev


def _up_prompt(h2, w_up, layer, conv_w, tm, tn, seq):
    m, k = h2.shape
    nf = w_up.shape[2] // 2
    nj = nf // tn
    kern = functools.partial(_up_prompt_kernel, tiles_per_seq=seq // tm)
    return pl.pallas_call(
        kern,
        grid=(nj, m // tm),
        in_specs=[pl.BlockSpec((tm, k), lambda j, i: (i, 0)),
                  pl.BlockSpec((None, k, tn), lambda j, i: (layer, 0, j)),
                  pl.BlockSpec((None, k, tn), lambda j, i: (layer, 0, nj + j)),
                  pl.BlockSpec((3, tn), lambda j, i: (0, j))],
        out_specs=[pl.BlockSpec((tm, tn), lambda j, i: (i, j)),
                   pl.BlockSpec((1, 8, tn), lambda j, i: ((i * tm) // seq, 0, j))],
        out_shape=[jax.ShapeDtypeStruct((m, nf), BF16),
                   jax.ShapeDtypeStruct((m // seq, 8, nf), F32)],
        scratch_shapes=[pltpu.VMEM((8, tn), F32), pltpu.VMEM((k, 2 * tn), BF16)],
        compiler_params=_cp("arbitrary", "arbitrary"),
        name="up_prompt",
    )(h2, w_up, w_up, conv_w)


def _up_sample_kernel(x_ref, wg_ref, wv_ref, cw_ref, b0_ref, b1_ref, act_ref, g_ref):
    x = x_ref[...]
    g = _dg(x, wg_ref[...].astype(BF16))
    v = _dg(x, wv_ref[...].astype(BF16))
    cw = cw_ref[...]
    hg = b0_ref[...] * cw[0:1, :] + b1_ref[...] * cw[1:2, :] + g * cw[2:3, :]
    act_ref[...] = (_silu(hg) * v).astype(act_ref.dtype)
    g_ref[...] = g


def _up_sample(h2, w_up, layer, conv_w, buf0, buf1, tn):
    m, k = h2.shape
    nf = w_up.shape[2] // 2
    nj = nf // tn
    row_spec = pl.BlockSpec((m, tn), lambda j: (0, j))
    return pl.pallas_call(
        _up_sample_kernel,
        grid=(nj,),
        in_specs=[pl.BlockSpec((m, k), lambda j: (0, 0)),
                  pl.BlockSpec((None, k, tn), lambda j: (layer, 0, j)),
                  pl.BlockSpec((None, k, tn), lambda j: (layer, 0, nj + j)),
                  pl.BlockSpec((3, tn), lambda j: (0, j)),
                  row_spec, row_spec],
        out_specs=[row_spec, row_spec],
        out_shape=[jax.ShapeDtypeStruct((m, nf), BF16), jax.ShapeDtypeStruct((m, nf), F32)],
        compiler_params=_cp("arbitrary"),
        name="up_sample",
    )(h2, w_up, w_up, conv_w, buf0, buf1)


def _head_scalars(ps, alog_row, dt_row, fb_row):
    lane = _iota(ps.shape, 1)
    beta = jax.nn.sigmoid(ps)
    g = -jnp.exp(alog_row) * jax.nn.softplus(ps + dt_row)
    logf = jax.nn.log_sigmoid(ps + fb_row)
    return jnp.where(lane < L_G, beta, jnp.where(lane < L_F, g, logf))


def _aux_kernel(ps_ref, alog_ref, dt_ref, fb_ref, a1_ref, a2_ref, ft_ref):
    t = ps_ref.shape[1]
    a1 = _head_scalars(ps_ref[0], alog_ref[...], dt_ref[...], fb_ref[...])
    a1_ref[0] = a1
    c = GDN_CHUNK
    tril = (_iota((c, c), 0) >= _iota((c, c), 1)).astype(BF16)
    keep = (_iota((1, HEAD_DIM), 1) >= L_F).astype(F32)
    carry = jnp.zeros((1, HEAD_DIM), F32)
    for ci in range(t // c):
        cs = _mm_exact_lhs(tril, a1[ci * c:(ci + 1) * c, :]) + carry * keep
        a2_ref[0, ci * c:(ci + 1) * c, :] = cs
        carry = cs[c - 1:c, :]
    a2t = a2_ref[0].T
    ft_ref[0] = a2t[L_F:L_F + N_HEADS, :]


def _aux(ps3, alog_row, dt_row, fb_row):
    b, t, n = ps3.shape
    row = pl.BlockSpec((1, n), lambda i: (0, 0))
    blk = pl.BlockSpec((1, t, n), lambda i: (i, 0, 0))
    return pl.pallas_call(
        _aux_kernel,
        grid=(b,),
        in_specs=[blk, row, row, row],
        out_specs=[blk, blk, pl.BlockSpec((1, N_HEADS, t), lambda i: (i, 0, 0))],
        out_shape=[jax.ShapeDtypeStruct((b, t, n), F32), jax.ShapeDtypeStruct((b, t, n), F32),
                   jax.ShapeDtypeStruct((b, N_HEADS, t), F32)],
        compiler_params=_cp("arbitrary"),
        name="aux",
    )(ps3, alog_row, dt_row, fb_row)


GDN_HG = 2


GDN_INV_BLOCK = 16
GDN_PREP_CHUNKS = 8
GDN_CONV_ROWS = 128


def _unit_lower_inverses(lows):
    n = lows[0].shape[0]
    bsz = GDN_INV_BLOCK
    ri = _iota((n, n), 0)
    ci = _iota((n, n), 1)
    eye = (ri == ci).astype(F32)
    nb = n // 8
    in_diag = ri // bsz == ci // bsz
    xbs = [[eye[8 * r:8 * r + 8, :] for r in range(nb)] for _ in lows]
    lbs = [[jnp.where(in_diag, low, 0.0)[8 * r:8 * r + 8, :] for r in range(nb)] for low in lows]
    for j in range(bsz - 1):
        for xb, lb in zip(xbs, lbs):
            for b in range(n // bsz):
                col = b * bsz + j
                rj = xb[col // 8][col % 8:col % 8 + 1, :]
                for r in range(col // 8, (b + 1) * bsz // 8):
                    xb[r] = xb[r] - lb[r][:, col:col + 1] * rj
    invs = [jnp.concatenate(xb, axis=0) for xb in xbs]
    s = bsz
    while s < n:
        pair = ri // (2 * s) == ci // (2 * s)
        same = ri // s == ci // s
        belows = [jnp.where(pair, jnp.where(same, 0.0, low), 0.0) for low in lows]
        left = [_mm3(inv, below) for inv, below in zip(invs, belows)]
        invs = [inv - _mm3(lt, inv) for inv, lt in zip(invs, left)]
        s *= 2
    return invs


def _gdn_prompt_kernel(q_ref, k_ref, v_ref, z_ref, wq_ref, wk_ref, wv_ref, a1_ref, a2_ref, nw_ref,
                       o_ref, s_ref, qs, ks, vs, qp, op, ap, bp, egl):
    t = q_ref.shape[0]
    c = GDN_CHUNK
    nc = t // c
    hbase = pl.program_id(1) * GDN_HG

    rb = min(GDN_CONV_ROWS, t)

    def conv_block(ref, w_ref_, sl, blk):
        if blk == 0:
            x = jnp.concatenate([jnp.zeros((8, HEAD_DIM), F32), ref[0:rb, sl]], axis=0)
        else:
            x = ref[blk * rb - 8:(blk + 1) * rb, sl]
        w = w_ref_[:, sl]
        width = w.shape[0]
        y = x * w[width - 1:width, :]
        for s in range(1, width):
            y = y + pltpu.roll(x, s, axis=0) * w[width - 1 - s:width - s, :]
        return _silu(y[8:, :])

    for h in range(GDN_HG):
        sl = slice(h * HEAD_DIM, (h + 1) * HEAD_DIM)
        for blk in range(t // rb):
            rows = slice(blk * rb, (blk + 1) * rb)
            q = conv_block(q_ref, wq_ref, sl, blk)
            k = conv_block(k_ref, wk_ref, sl, blk)
            qs[h, rows, :] = q * lax.rsqrt(jnp.sum(q * q, axis=-1, keepdims=True) + EPS) * (HEAD_DIM ** -0.5)
            ks[h, rows, :] = k * lax.rsqrt(jnp.sum(k * k, axis=-1, keepdims=True) + EPS)
            vs[h, rows, :] = conv_block(v_ref, wv_ref, sl, blk)

    ri = _iota((c, c), 0)
    ci_ = _iota((c, c), 1)
    causal = ri >= ci_
    strict = ri > ci_
    eye = ri == ci_

    def prep(it, carry):
        chains = [(it * GDN_PREP_CHUNKS + cc, h) for cc in range(GDN_PREP_CHUNKS) for h in range(GDN_HG)]
        loaded = []
        for ci, h in chains:
            rows = pl.ds(pl.multiple_of(ci * c, c), c)
            loaded.append((a1_ref[0, rows, :], a2_ref[0, rows, :], qs[h, rows, :], ks[h, rows, :], vs[h, rows, :]))
        nch = range(len(chains))
        beta = [_lane_pick(loaded[i][0], L_BETA + hbase + chains[i][1]) for i in nch]
        gc = [_lane_pick(loaded[i][1], L_G + hbase + chains[i][1]) for i in nch]
        qv = [loaded[i][2] for i in nch]
        kv = [loaded[i][3] for i in nch]
        vv = [loaded[i][4] for i in nch]
        decay = []
        for i in nch:
            gc_row = jnp.sum(jnp.where(eye, gc[i], 0.0), axis=0, keepdims=True)
            decay.append(jnp.where(causal, jnp.exp(jnp.where(causal, gc[i] - gc_row, 0.0)), 0.0))
        kb = [kv[i] * beta[i] for i in nch]
        lows = [_mm1(kb[i], kv[i], NT) * jnp.where(strict, decay[i], 0.0) for i in nch]
        qk = [_mm1(qv[i], kv[i], NT) * decay[i] for i in nch]
        invs = _unit_lower_inverses(lows)
        eg = [jnp.exp(gc[i]) for i in nch]
        wu = [_mm3(invs[i], jnp.concatenate([kb[i] * eg[i], vv[i] * beta[i]], axis=1)) for i in nch]
        g_last = [gc[i][c - 1:c, :] for i in nch]
        kd = [kv[i] * jnp.exp(g_last[i] - gc[i]) for i in nch]
        kwu = [_mm1(kd[i], wu[i], TN) for i in nch]
        qwu = [_mm1(qk[i], wu[i]) for i in nch]
        results = [((qv[i] * eg[i] - qwu[i][:, :HEAD_DIM]).astype(BF16), qwu[i][:, HEAD_DIM:],
                    (-kwu[i][:, :HEAD_DIM]).astype(BF16), kwu[i][:, HEAD_DIM:],
                    jnp.broadcast_to(jnp.exp(g_last[i]), (8, HEAD_DIM))) for i in nch]
        for (ci, h), (qp_c, op_c, ap_c, bp_c, e_c) in zip(chains, results):
            rows = pl.ds(pl.multiple_of(ci * c, c), c)
            srows = pl.ds(pl.multiple_of(ci * HEAD_DIM, HEAD_DIM), HEAD_DIM)
            qp[h, rows, :] = qp_c
            op[h, rows, :] = op_c
            ap[h, srows, :] = ap_c
            bp[h, srows, :] = bp_c
            egl[h, pl.ds(pl.multiple_of(ci * 8, 8), 8), :] = e_c
        return carry

    lax.fori_loop(0, nc // GDN_PREP_CHUNKS, prep, 0)

    nw = nw_ref[...]

    def sweep(ci, states):
        rows = pl.ds(pl.multiple_of(ci * c, c), c)
        srows = pl.ds(pl.multiple_of(ci * HEAD_DIM, HEAD_DIM), HEAD_DIM)
        new_states = []
        for h in range(GDN_HG):
            s = states[h]
            sb = s.astype(BF16)
            o = _dg(qp[h, rows, :], sb) + op[h, rows, :]
            e_last = egl[h, pl.ds(pl.multiple_of(ci * 8, 8), 8), :][0:1, :]
            new_states.append(s * e_last + (_dg(ap[h, srows, :], sb) + bp[h, srows, :]))
            o = o * lax.rsqrt(jnp.mean(o * o, axis=-1, keepdims=True) + EPS) * nw
            z = z_ref[rows, h * HEAD_DIM:(h + 1) * HEAD_DIM]
            o_ref[rows, h * HEAD_DIM:(h + 1) * HEAD_DIM] = (o * _silu(z)).astype(o_ref.dtype)
        return tuple(new_states)

    init = tuple(jnp.zeros((HEAD_DIM, HEAD_DIM), F32) for _ in range(GDN_HG))
    final = lax.fori_loop(0, nc, sweep, init)
    for h in range(GDN_HG):
        s_ref[0, h] = final[h]


def _gdn_prompt(proj, a1, a2, conv_w, norm_w, b, t):
    hw = GDN_HG * HEAD_DIM
    ng = W_GROUP // hw
    col = lambda off: pl.BlockSpec((t, hw), lambda i, j: (i, off * ng + j))
    wcol = lambda off: pl.BlockSpec((4, hw), lambda i, j: (0, off * ng + j))
    aux = pl.BlockSpec((1, t, HEAD_DIM), lambda i, j: (i, 0, 0))
    nc = t // GDN_CHUNK
    scr = lambda rows, dt: pltpu.VMEM((GDN_HG, rows, HEAD_DIM), dt)
    return pl.pallas_call(
        _gdn_prompt_kernel,
        grid=(b, ng),
        in_specs=[col(0), col(1), col(2), col(3), wcol(0), wcol(1), wcol(2), aux, aux,
                  pl.BlockSpec((1, HEAD_DIM), lambda i, j: (0, 0))],
        out_specs=[pl.BlockSpec((t, hw), lambda i, j: (i, j)),
                   pl.BlockSpec((1, GDN_HG, HEAD_DIM, HEAD_DIM), lambda i, j: (i, j, 0, 0))],
        out_shape=[jax.ShapeDtypeStruct((b * t, W_GROUP), BF16),
                   jax.ShapeDtypeStruct((b, N_HEADS, HEAD_DIM, HEAD_DIM), F32)],
        scratch_shapes=[scr(t, F32), scr(t, F32), scr(t, F32), scr(t, BF16), scr(t, F32),
                        scr(nc * HEAD_DIM, BF16), scr(nc * HEAD_DIM, F32), scr(nc * 8, F32)],
        compiler_params=_cp("arbitrary", "arbitrary"),
        name="gdn_prompt",
    )(proj, proj, proj, proj, conv_w, conv_w, conv_w, a1, a2, norm_w.reshape(1, HEAD_DIM))


def _sconv_prompt_kernel(x_ref, gb_ref, gc_ref, w_ref, o_ref, tail_ref):
    u = gc_ref[...] * x_ref[...]
    t = u.shape[0]
    o_ref[...] = (gb_ref[...] * _causal_conv(u, w_ref[...])).astype(o_ref.dtype)
    tail_ref[0] = u[t - 8:t, :]


def _sconv_prompt(proj, w, b, t):
    tc = 256
    nb = W_GROUP // tc
    base = M_B // tc
    col = lambda k: pl.BlockSpec((t, tc), lambda i, j: (i, base + k * nb + j))
    return pl.pallas_call(
        _sconv_prompt_kernel,
        grid=(b, nb),
        in_specs=[col(0), col(1), col(2), pl.BlockSpec((3, tc), lambda i, j: (0, j))],
        out_specs=[pl.BlockSpec((t, tc), lambda i, j: (i, j)),
                   pl.BlockSpec((1, 8, tc), lambda i, j: (i, 0, j))],
        out_shape=[jax.ShapeDtypeStruct((b * t, W_GROUP), BF16),
                   jax.ShapeDtypeStruct((b, 8, W_GROUP), F32)],
        compiler_params=_cp("arbitrary", "arbitrary"),
        name="sconv_prompt",
    )(proj, proj, proj, w)


FOX_TQ = 512


def _fox_prompt_kernel(q_ref, k_ref, v_ref, ft_ref, o_ref):
    h = pl.program_id(1)
    qi = pl.program_id(2)
    q = (q_ref[...] * (HEAD_DIM ** -0.5)).astype(BF16)
    tq = q.shape[0]
    tk = tq
    hsel = _iota((N_HEADS, tk), 0) == h

    def tile(j, carry, diagonal):
        m, l, acc = carry
        start = pl.multiple_of(j * tk, tk)
        k = k_ref[pl.ds(start, tk), :].astype(BF16)
        v = v_ref[pl.ds(start, tk), :].astype(BF16)
        fk = jnp.sum(jnp.where(hsel, ft_ref[0, :, pl.ds(start, tk)], 0.0), axis=0, keepdims=True)
        s = _dg(q, k, NT) - fk
        if diagonal:
            s = jnp.where(_iota((tq, tk), 1) <= _iota((tq, tk), 0), s, -jnp.inf)
        m_new = jnp.maximum(m, jnp.max(s, axis=-1, keepdims=True))
        alpha = jnp.exp(m - m_new)
        p = jnp.exp(s - m_new)
        l = alpha * l + jnp.sum(p, axis=-1, keepdims=True)
        acc = alpha * acc + _dg(p.astype(BF16), v)
        return m_new, l, acc

    init = (jnp.full((tq, 1), -jnp.inf, F32), jnp.zeros((tq, 1), F32), jnp.zeros((tq, HEAD_DIM), F32))
    carry = lax.fori_loop(0, qi, lambda j, c: tile(j, c, False), init)
    _, l, acc = tile(qi, carry, True)
    o_ref[...] = (acc / l).astype(o_ref.dtype)


def _fox_prompt(proj, ft, b, t):
    tq = min(FOX_TQ, t)
    nq = t // tq
    base = M_C // HEAD_DIM
    return pl.pallas_call(
        _fox_prompt_kernel,
        grid=(b, N_HEADS, nq),
        in_specs=[pl.BlockSpec((tq, HEAD_DIM), lambda i, h, q: (i * nq + q, base + h)),
                  pl.BlockSpec((t, HEAD_DIM), lambda i, h, q: (i, base + N_HEADS + h)),
                  pl.BlockSpec((t, HEAD_DIM), lambda i, h, q: (i, base + 2 * N_HEADS + h)),
                  pl.BlockSpec((1, N_HEADS, t), lambda i, h, q: (i, 0, 0))],
        out_specs=pl.BlockSpec((tq, HEAD_DIM), lambda i, h, q: (i * nq + q, h)),
        out_shape=jax.ShapeDtypeStruct((b * t, W_GROUP), BF16),
        compiler_params=_cp("arbitrary", "arbitrary", "arbitrary"),
        name="fox_prompt",
    )(proj, proj, proj, ft)


def _lru_gates(xd, wr, br, wi, bi, lam):
    r = jax.nn.sigmoid(_mm3(xd, wr) + br)
    i = jax.nn.sigmoid(_mm3(xd, wi) + bi)
    log_a = -LRU_C * r * jax.nn.softplus(-lam)
    a = jnp.exp(log_a)
    th = jnp.tanh(log_a)
    one_minus_a2 = -2.0 * th / (1.0 - th)
    return a, jnp.sqrt(one_minus_a2) * (i * xd)


def _lru_prompt_kernel(x_ref, y_ref, cw_ref, cb_ref, wr_ref, br_ref, wi_ref, bi_ref, lam_ref,
                       o_ref, tail_ref):
    t = x_ref.shape[0]
    xd = _causal_conv(x_ref[...], cw_ref[...]) + cb_ref[...]
    a, bb = _lru_gates(xd, wr_ref[0], br_ref[...], wi_ref[0], bi_ref[...], lam_ref[...])
    row = _iota(a.shape, 0)
    s = 1
    while s < t:
        a_s = jnp.where(row < s, 1.0, pltpu.roll(a, s, axis=0))
        b_s = jnp.where(row < s, 0.0, pltpu.roll(bb, s, axis=0))
        bb = a * b_s + bb
        a = a * a_s
        s *= 2
    o_ref[...] = (jax.nn.gelu(y_ref[...]) * bb).astype(o_ref.dtype)
    tail_ref[0] = bb[t - 8:t, :]


def _lru_prompt(proj, cw, cb, wr, br, wi, bi, lam, b, t):
    n = HEAD_DIM
    nb = W_GROUP // n
    base = M_D // n
    vec = lambda: pl.BlockSpec((1, n), lambda i, j: (0, j))
    mat = lambda: pl.BlockSpec((1, n, n), lambda i, j: (j, 0, 0))
    r1 = lambda a: a.reshape(1, W_GROUP)
    return pl.pallas_call(
        _lru_prompt_kernel,
        grid=(b, nb),
        in_specs=[pl.BlockSpec((t, n), lambda i, j: (i, base + j)),
                  pl.BlockSpec((t, n), lambda i, j: (i, base + nb + j)),
                  pl.BlockSpec((4, n), lambda i, j: (0, j)),
                  vec(), mat(), vec(), mat(), vec(), vec()],
        out_specs=[pl.BlockSpec((t, n), lambda i, j: (i, j)),
                   pl.BlockSpec((1, 8, n), lambda i, j: (i, 0, j))],
        out_shape=[jax.ShapeDtypeStruct((b * t, W_GROUP), BF16),
                   jax.ShapeDtypeStruct((b, 8, W_GROUP), F32)],
        compiler_params=_cp("arbitrary", "arbitrary"),
        name="lru_prompt",
    )(proj, proj, cw, r1(cb), wr, r1(br), wi, r1(bi), r1(lam))


def _gdn_sample_kernel(q_ref, k_ref, v_ref, z_ref, b0_ref, b1_ref, b2_ref, cw_ref, ps_ref,
                       alog_ref, dt_ref, fb_ref, nw_ref, s_ref, o_ref, so_ref, hs_ref):
    i = pl.program_id(0)
    cw = cw_ref[...]
    hs = _head_scalars(ps_ref[...], alog_ref[...], dt_ref[...], fb_ref[...])
    hs_ref[...] = hs
    hs_i = jnp.sum(jnp.where(_iota(hs.shape, 0) == i, hs, 0.0), axis=0, keepdims=True)
    row_i = lambda ref: ref[pl.ds(i, 1), :]
    b0 = row_i(b0_ref)
    b1 = row_i(b1_ref)
    b2 = row_i(b2_ref)
    eye = _iota((HEAD_DIM, HEAD_DIM), 0) == _iota((HEAD_DIM, HEAD_DIM), 1)
    nw = nw_ref[...]

    def conv(x_row, off):
        w = cw[:, off:off + W_GROUP]
        bo = slice(off, off + W_GROUP)
        return _silu(b0[:, bo] * w[0:1, :] + b1[:, bo] * w[1:2, :] + b2[:, bo] * w[2:3, :] + x_row * w[3:4, :])

    q_all = conv(row_i(q_ref), 0)
    k_all = conv(row_i(k_ref), W_GROUP)
    v_all = conv(row_i(v_ref), 2 * W_GROUP)
    z_all = row_i(z_ref)
    for h in range(N_HEADS):
        sl = slice(h * HEAD_DIM, (h + 1) * HEAD_DIM)
        q = q_all[:, sl]
        k = k_all[:, sl]
        v = v_all[:, sl]
        q = q * lax.rsqrt(jnp.sum(q * q, axis=-1, keepdims=True) + EPS) * (HEAD_DIM ** -0.5)
        k = k * lax.rsqrt(jnp.sum(k * k, axis=-1, keepdims=True) + EPS)
        beta = hs_i[:, L_BETA + h:L_BETA + h + 1]
        g = hs_i[:, L_G + h:L_G + h + 1]
        k_col = jnp.sum(jnp.where(eye, k, 0.0), axis=1, keepdims=True)
        q_col = jnp.sum(jnp.where(eye, q, 0.0), axis=1, keepdims=True)
        s = s_ref[0, h] * jnp.exp(g)
        delta = (v - jnp.sum(k_col * s, axis=0, keepdims=True)) * beta
        s = s + k_col * delta
        so_ref[0, h] = s
        o = jnp.sum(q_col * s, axis=0, keepdims=True)
        o = o * lax.rsqrt(jnp.mean(o * o, axis=-1, keepdims=True) + EPS) * nw
        o_ref[0, :, sl] = o * _silu(z_all[:, sl])


def _gdn_sample(proj, ps, buf, conv_w, alog_row, dt_row, fb_row, norm_w, state, bs):
    rows = proj.shape[0]
    col = lambda k: pl.BlockSpec((rows, W_GROUP), lambda i: (0, k))
    full = lambda a: pl.BlockSpec(a.shape, lambda i: (0,) * a.ndim)
    prow = pl.BlockSpec((1, HEAD_DIM), lambda i: (0, 0))
    sblk = pl.BlockSpec((1, N_HEADS, HEAD_DIM, HEAD_DIM), lambda i: (i, 0, 0, 0))
    return pl.pallas_call(
        _gdn_sample_kernel,
        grid=(bs,),
        in_specs=[col(0), col(1), col(2), col(3), full(buf[0]), full(buf[1]), full(buf[2]),
                  full(conv_w), pl.BlockSpec((rows, HEAD_DIM), lambda i: (0, 0)), prow, prow, prow, prow, sblk],
        out_specs=[pl.BlockSpec((1, 1, W_GROUP), lambda i: (i, 0, 0)), sblk,
                   pl.BlockSpec((rows, HEAD_DIM), lambda i: (0, 0))],
        out_shape=[jax.ShapeDtypeStruct((bs, 1, W_GROUP), F32),
                   jax.ShapeDtypeStruct(state.shape, F32),
                   jax.ShapeDtypeStruct((rows, HEAD_DIM), F32)],
        compiler_params=_cp("arbitrary"),
        name="gdn_sample",
    )(proj, proj, proj, proj, buf[0], buf[1], buf[2], conv_w, ps, alog_row, dt_row, fb_row,
      norm_w.reshape(1, HEAD_DIM), state)


def _bd_sample_kernel(xb_ref, gb_ref, gc_ref, sb0_ref, sb1_ref, sw_ref,
                      xd_ref, yd_ref, lb0_ref, lb1_ref, lb2_ref, lh_ref, cw_ref, cb_ref,
                      wr_ref, br_ref, wi_ref, bi_ref, lam_ref,
                      ob_ref, u_ref, od_ref, h_ref):
    sw = sw_ref[...]
    u = gc_ref[...] * xb_ref[...]
    ob_ref[...] = gb_ref[...] * (sb0_ref[...] * sw[0:1, :] + sb1_ref[...] * sw[1:2, :] + u * sw[2:3, :])
    u_ref[...] = u
    cw = cw_ref[...]
    xd = (lb0_ref[...] * cw[0:1, :] + lb1_ref[...] * cw[1:2, :] + lb2_ref[...] * cw[2:3, :]
          + xd_ref[...] * cw[3:4, :] + cb_ref[...])
    n = HEAD_DIM
    for j in range(W_GROUP // n):
        sl = slice(j * n, (j + 1) * n)
        a, bb = _lru_gates(xd[:, sl], wr_ref[j], br_ref[:, sl], wi_ref[j], bi_ref[:, sl], lam_ref[:, sl])
        hn = a * lh_ref[:, sl] + bb
        h_ref[:, sl] = hn
        od_ref[:, sl] = jax.nn.gelu(yd_ref[:, sl]) * hn


def _bd_sample(proj, sbuf, sw, lbuf, lh, cw, cb, wr, br, wi, bi, lam):
    rows = proj.shape[0]
    r1 = lambda a: a.reshape(1, W_GROUP)
    pcol = lambda c: pl.BlockSpec((rows, W_GROUP), lambda i: (0, c // W_GROUP))
    full = lambda a: pl.BlockSpec(a.shape, lambda i: (0,) * a.ndim)
    ins = [proj, proj, proj, sbuf[0], sbuf[1], sw, proj, proj, lbuf[0], lbuf[1], lbuf[2], lh, cw,
           r1(cb), wr, r1(br), wi, r1(bi), r1(lam)]
    specs = [pcol(M_B), pcol(M_B + W_GROUP), pcol(M_B + 2 * W_GROUP)] + [full(a) for a in ins[3:6]] \
        + [pcol(M_D), pcol(M_D + W_GROUP)] + [full(a) for a in ins[8:]]
    out = jax.ShapeDtypeStruct((rows, W_GROUP), F32)
    ospec = pl.BlockSpec((rows, W_GROUP), lambda i: (0, 0))
    return pl.pallas_call(
        _bd_sample_kernel,
        grid=(1,),
        in_specs=specs,
        out_specs=[ospec] * 4,
        out_shape=[out] * 4,
        compiler_params=_cp("arbitrary"),
        name="bd_sample",
    )(*ins)


FOX_PAGES_PER_STEP = 8


def _fox_sample_kernel(pt_ref, q_ref, kn_ref, vn_ref, r0_ref, *refs):
    del pt_ref
    n = FOX_PAGES_PER_STEP
    kp_refs, vp_refs, lf_refs = refs[0:n], refs[n:2 * n], refs[2 * n:3 * n]
    o_ref, m_sc, l_sc, acc_sc, r_sc = refs[3 * n:]
    i = pl.program_id(0)
    p = pl.program_id(1)
    npg = pl.num_programs(1)
    scale = HEAD_DIM ** -0.5
    q = q_ref[i]

    @pl.when(p == 0)
    def _():
        m_sc[...] = jnp.sum(q * kn_ref[i], axis=-1, keepdims=True) * scale
        l_sc[...] = jnp.ones_like(l_sc)
        acc_sc[...] = vn_ref[i]
        r_sc[...] = r0_ref[pl.ds(i, 1), :]

    m = m_sc[...]
    l = l_sc[...]
    acc = acc_sc[...]
    r = r_sc[...]
    pg, nh, hd = kp_refs[0].shape
    flat = pg * nh
    qb = q.astype(BF16)
    lane = _iota((1, flat), 1)
    own = _iota((nh, flat), 1) % nh == _iota((nh, flat), 0)
    pages = range(n)
    lfs = [lf_refs[t][...] for t in pages]
    suffix = list(lfs)
    total = list(lfs)
    step = nh
    while step < flat:
        suffix = [x + jnp.where(lane + step < flat, pltpu.roll(x, flat - step, axis=1), 0.0) for x in suffix]
        total = [x + pltpu.roll(x, step, axis=1) for x in total]
        step *= 2
    scores = [_dg(qb, kp_refs[t][...].reshape(flat, hd).astype(BF16), NT) * scale for t in pages]
    s = []
    for t in pages:
        s.append(jnp.where(own, scores[t] + ((suffix[t] - lfs[t]) + r), -jnp.inf))
        r = r + total[t]
    m_new = m
    for t in pages:
        m_new = jnp.maximum(m_new, jnp.max(s[t], axis=-1, keepdims=True))
    alpha = jnp.exp(m - m_new)
    pw = [jnp.exp(s[t] - m_new) for t in pages]
    l = alpha * l
    acc = alpha * acc
    for t in pages:
        l = l + jnp.sum(pw[t], axis=-1, keepdims=True)
        acc = acc + _dg(pw[t].astype(BF16), vp_refs[t][...].reshape(flat, hd).astype(BF16))
    m_sc[...] = m_new
    l_sc[...] = l
    acc_sc[...] = acc
    r_sc[...] = r

    @pl.when(p == npg - 1)
    def _():
        o_ref[0] = acc / l


def _fox_sample(q3, kn3, vn3, logf_new, page_table, ck, cv, clf, layer, bs):
    rows = q3.shape[0]
    n_pages = page_table.shape[1]
    page = ck.shape[2]
    flat = page * N_HEADS
    n = FOX_PAGES_PER_STEP
    assert n_pages % n == 0
    r0 = jnp.tile(logf_new, (1, page))
    clf_flat = clf.reshape(clf.shape[0], clf.shape[1], 1, flat)
    new_spec = pl.BlockSpec((rows, N_HEADS, HEAD_DIM), lambda i, p, pt: (0, 0, 0))

    def kv_spec(t):
        return pl.BlockSpec((None, None, page, N_HEADS, HEAD_DIM),
                            lambda i, p, pt: (layer, pt[i, n_pages - 1 - (p * n + t)], 0, 0, 0))

    def lf_spec(t):
        return pl.BlockSpec((None, None, 1, flat),
                            lambda i, p, pt: (layer, pt[i, n_pages - 1 - (p * n + t)], 0, 0))

    slots = list(range(n))
    grid_spec = pltpu.PrefetchScalarGridSpec(
        num_scalar_prefetch=1,
        grid=(bs, n_pages // n),
        in_specs=[new_spec, new_spec, new_spec, pl.BlockSpec((rows, flat), lambda i, p, pt: (0, 0))]
        + [kv_spec(t) for t in slots] + [kv_spec(t) for t in slots] + [lf_spec(t) for t in slots],
        out_specs=pl.BlockSpec((1, N_HEADS, HEAD_DIM), lambda i, p, pt: (i, 0, 0)),
        scratch_shapes=[pltpu.VMEM((N_HEADS, 1), F32), pltpu.VMEM((N_HEADS, 1), F32),
                        pltpu.VMEM((N_HEADS, HEAD_DIM), F32), pltpu.VMEM((1, flat), F32)],
    )
    return pl.pallas_call(
        _fox_sample_kernel,
        grid_spec=grid_spec,
        out_shape=jax.ShapeDtypeStruct((bs, N_HEADS, HEAD_DIM), F32),
        compiler_params=_cp("arbitrary", "arbitrary"),
        name="fox_sample",
    )(page_table, q3, kn3, vn3, r0, *([ck] * n), *([cv] * n), *([clf_flat] * n))


REGROUP_TN = 512
REGROUP_ROWS = 512


def _regroup_kernel(a_ref, t_ref, s1_ref, s2_ref, o_ref, small_ref, *, tiles_a, tiles_bc):
    c = pl.program_id(1)
    k, tn = a_ref.shape

    @pl.when(c == 0)
    def _():
        lane = _iota(s1_ref.shape, 1)
        small = jnp.where(lane < L_F, s1_ref[...], jnp.where(lane < L_F + N_HEADS, s2_ref[...], 0.0))
        small_ref[...] = small.astype(BF16)

    def copy_shifted(delta):
        for r in range(k // REGROUP_ROWS):
            rows = slice(r * REGROUP_ROWS, (r + 1) * REGROUP_ROWS)
            if delta == 0:
                o_ref[rows, :] = a_ref[rows, :].astype(BF16)
            else:
                cat = jnp.concatenate([a_ref[rows, :], t_ref[rows, :]], axis=1)
                o_ref[rows, :] = pltpu.roll(cat, cat.shape[1] - delta, axis=1)[:, 0:tn].astype(BF16)

    @pl.when(c < tiles_a)
    def _():
        copy_shifted(0)

    @pl.when(jnp.logical_and(c >= tiles_a, c < tiles_a + tiles_bc))
    def _():
        copy_shifted(OFF_B - M_B)

    @pl.when(c >= tiles_a + tiles_bc)
    def _():
        copy_shifted(OFF_D - M_D)


def _regroup_w_in(w_in):
    depth, k, _ = w_in.shape
    tn = REGROUP_TN
    tail = HEAD_DIM
    assert OFF_A_BETA % tail == 0 and OFF_C_F - (OFF_C_F // tail) * tail == L_F
    return pl.pallas_call(
        functools.partial(_regroup_kernel, tiles_a=M_B // tn, tiles_bc=(M_D - M_B) // tn),
        grid=(depth, N_MAIN // tn),
        in_specs=[pl.BlockSpec((None, k, tn), lambda l, c: (l, 0, c)),
                  pl.BlockSpec((None, k, tail), lambda l, c: (l, 0, (c + 1) * (tn // tail))),
                  pl.BlockSpec((None, k, tail), lambda l, c: (l, 0, OFF_A_BETA // tail)),
                  pl.BlockSpec((None, k, tail), lambda l, c: (l, 0, OFF_C_F // tail))],
        out_specs=[pl.BlockSpec((None, k, tn), lambda l, c: (l, 0, c)),
                   pl.BlockSpec((None, k, tail), lambda l, c: (l, 0, 0))],
        out_shape=[jax.ShapeDtypeStruct((depth, k, N_MAIN), BF16),
                   jax.ShapeDtypeStruct((depth, k, tail), BF16)],
        compiler_params=_cp("arbitrary", "arbitrary"),
        name="regroup_w_in",
    )(w_in, w_in, w_in, w_in)


def _prep_weights(w_in, w_out, w_up, w_down):
    w_main, w_small = _regroup_w_in(w_in)
    return dict(w_main=w_main, w_small=w_small, w_out=w_out.astype(BF16), w_up=w_up,
                w_down=w_down.astype(BF16))


def _lane_row(v, off):
    return jnp.zeros((1, HEAD_DIM), F32).at[0, off:off + N_HEADS].set(v)


def _pad_rows(a, rows):
    return jnp.pad(a, ((0, rows - a.shape[0]),) + ((0, 0),) * (a.ndim - 1))


def kernel(x_prompt, x_sample, cache_k, cache_v, cache_logf, state_gdn, state_gdn_conv, state_sconv,
           state_lru_h, state_lru_conv, state_ffn_conv, page_table, c_prompt, c_sample, ada_w, ada_b,
           ln1_w, ln2_w, w_in, gdn_conv_w, gdn_a_log, gdn_dt_bias, gdn_norm_w, sconv_w, fox_f_bias,
           lru_conv_w, lru_conv_b, lru_wr, lru_br, lru_wi, lru_bi, lru_lambda, w_out, ffn_conv_w,
           w_up, w_down, final_norm_w):
    bp, t, d = x_prompt.shape
    bs = x_sample.shape[0]
    depth = w_in.shape[0]
    sr = SAMPLE_ROWS
    tm = min(1024, t)
    tnorm = min(512, t)
    c16 = _pad_rows(jnp.concatenate([c_prompt, c_sample], axis=0), sr)

    xp = x_prompt.reshape(bp * t, d)
    xs = _pad_rows(x_sample.reshape(bs, d), sr)
    p_states, s_states = [], []
    w = _prep_weights(w_in, w_out, w_up, w_down)
    for l in range(depth):
        alog_row = _lane_row(gdn_a_log[l], L_G)
        dt_row = _lane_row(gdn_dt_bias[l], L_G)
        fb_row = _lane_row(fox_f_bias[l], L_F)
        ffn_cw = ffn_conv_w[l]
        mod = _ada(c16, ada_w, ada_b, l).reshape(sr, 6, d)
        pm = [mod[0:bp, k][:, None, :] for k in range(6)]
        sm = [_pad_rows(mod[bp:bp + bs, k], sr)[None] for k in range(6)]

        h = _norm_mod(xp.reshape(bp, t, d), ln1_w[l], pm[1], pm[0], tnorm).reshape(bp * t, d)
        proj = _matmul(h, w["w_main"], l, tm, 1024, "mm_in")
        ps = _matmul(h, w["w_small"], l, tm, HEAD_DIM, "mm_in_small")
        a1, a2, ft = _aux(ps.reshape(bp, t, HEAD_DIM), alog_row, dt_row, fb_row)
        oa, p_gdn = _gdn_prompt(proj, a1, a2, gdn_conv_w[l], gdn_norm_w[l], bp, t)
        ob, sconv_tail = _sconv_prompt(proj, sconv_w[l], bp, t)
        oc = _fox_prompt(proj, ft, bp, t)
        od, lru_tail = _lru_prompt(proj, lru_conv_w[l], lru_conv_b[l], lru_wr[l], lru_br[l], lru_wi[l],
                                   lru_bi[l], lru_lambda[l], bp, t)
        xp = _mm_out([oa, ob, oc, od], w["w_out"], l, xp, pm[2], tm, 1024, t)
        h2 = _norm_mod(xp.reshape(bp, t, d), ln2_w[l], pm[4], pm[3], tnorm).reshape(bp * t, d)
        act, ffn_tail = _up_prompt(h2, w["w_up"], l, ffn_cw, min(UP_TM, t), UP_TN, t)
        xp = _mm_down(act, w["w_down"], l, xp, pm[5], min(512, t), 512, D_FF, t)

        proj3 = proj.reshape(bp, t, N_MAIN)
        p_states.append((
            proj3[:, :, M_C + W_GROUP:M_C + 2 * W_GROUP].reshape(bp, t, N_HEADS, HEAD_DIM),
            proj3[:, :, M_C + 2 * W_GROUP:M_C + 3 * W_GROUP].reshape(bp, t, N_HEADS, HEAD_DIM),
            a1[:, :, L_F:L_F + N_HEADS],
            p_gdn,
            proj3[:, t - 3:, M_A:M_A + 3 * W_GROUP],
            sconv_tail[:, 6:8, :],
            lru_tail[:, 7, :],
            proj3[:, t - 3:, M_D:M_D + W_GROUP],
            ffn_tail[:, 6:8, :],
        ))

        hs_in = _norm_mod(xs[None], ln1_w[l], sm[1], sm[0], sr)[0]
        sproj = _matmul(hs_in, w["w_main"], l, sr, 2048, "mm_in_s")
        sps = _matmul(hs_in, w["w_small"], l, sr, HEAD_DIM, "mm_in_small_s")
        gbuf = [_pad_rows(state_gdn_conv[l][:, j, :], sr) for j in range(3)]
        soa, s_gdn, shs = _gdn_sample(sproj, sps, gbuf, gdn_conv_w[l], alog_row, dt_row,
                                      fb_row, gdn_norm_w[l], state_gdn[l], bs)
        sbuf = [_pad_rows(state_sconv[l][:, j, :], sr) for j in range(2)]
        lbuf = [_pad_rows(state_lru_conv[l][:, j, :], sr) for j in range(3)]
        sob, s_u, sod, s_h = _bd_sample(sproj, sbuf, sconv_w[l], lbuf, _pad_rows(state_lru_h[l], sr),
                                        lru_conv_w[l], lru_conv_b[l], lru_wr[l], lru_br[l], lru_wi[l],
                                        lru_bi[l], lru_lambda[l])
        heads3 = lambda off: sproj[:, off:off + W_GROUP].reshape(sr, N_HEADS, HEAD_DIM)
        soc = _fox_sample(heads3(M_C), heads3(M_C + W_GROUP), heads3(M_C + 2 * W_GROUP),
                          shs[:, L_F:L_F + N_HEADS], page_table, cache_k, cache_v, cache_logf, l, bs)
        parts = [_pad_rows(soa.reshape(bs, W_GROUP), sr).astype(BF16), sob.astype(BF16),
                 _pad_rows(soc.reshape(bs, W_GROUP), sr).astype(BF16), sod.astype(BF16)]
        xs = _mm_out(parts, w["w_out"], l, xs, sm[2], sr, 2048, sr)
        sh2 = _norm_mod(xs[None], ln2_w[l], sm[4], sm[3], sr)[0]
        fbuf = [_pad_rows(state_ffn_conv[l][:, j, :], sr) for j in range(2)]
        sact, s_g = _up_sample(sh2, w["w_up"], l, ffn_cw, fbuf[0], fbuf[1], UP_TN)
        xs = _mm_down(sact, w["w_down"], l, xs, sm[5], sr, 512, D_FF, sr)

        sp = sproj[:bs]
        s_states.append((
            sp[:, M_C + W_GROUP:M_C + 2 * W_GROUP].reshape(bs, 1, N_HEADS, HEAD_DIM),
            sp[:, M_C + 2 * W_GROUP:M_C + 3 * W_GROUP].reshape(bs, 1, N_HEADS, HEAD_DIM),
            shs[:bs, L_F:L_F + N_HEADS].reshape(bs, 1, N_HEADS),
            s_gdn,
            jnp.concatenate([state_gdn_conv[l][:, 1:, :], sp[:, None, M_A:M_A + 3 * W_GROUP]], axis=1),
            jnp.concatenate([state_sconv[l][:, 1:, :], s_u[:bs, None, :]], axis=1),
            s_h[:bs],
            jnp.concatenate([state_lru_conv[l][:, 1:, :], sp[:, None, M_D:M_D + W_GROUP]], axis=1),
            jnp.concatenate([state_ffn_conv[l][:, 1:, :], s_g[:bs, None, :]], axis=1),
        ))

    y_prompt = _rmsnorm(xp, final_norm_w, tnorm).reshape(bp, t, d)
    y_sample = _rmsnorm(xs, final_norm_w, sr)[:bs].reshape(bs, 1, d)
    p_out = tuple(jnp.stack(parts) for parts in zip(*p_states))
    s_out = tuple(jnp.stack(parts) for parts in zip(*s_states))
    return (y_prompt, y_sample) + p_out + s_out
```
